```python
import math
import jax, jax.numpy as jnp
from jax import lax
import numpy as np

D_MODEL = 1024
BATCH = 8
SEQ = 2048
DEPTH = 1
DEC_BATCH = 16
DEC_SEQ = 2048
PAST_LEN = 128

N_HEADS = 8
QK_NOPE_DIM = 128
QK_ROPE_DIM = 64
QK_HEAD_DIM = QK_NOPE_DIM + QK_ROPE_DIM
V_HEAD_DIM = 128
Q_LORA_RANK = 384
KV_LORA_RANK = 256
ATTN_WIDTH = N_HEADS * V_HEAD_DIM
CONV_WIDTH = D_MODEL
CONV_KERNEL = 31
CONV_PAD = (CONV_KERNEL - 1) // 2
ROPE_THETA = 10000.0
Q_BLOCK = 128
EPS = 1e-6
SPLIT_SIZES = (Q_LORA_RANK, KV_LORA_RANK, QK_ROPE_DIM, ATTN_WIDTH, 2 * CONV_WIDTH, CONV_WIDTH, D_MODEL, D_MODEL)
D_IN = Q_LORA_RANK + KV_LORA_RANK + QK_ROPE_DIM + ATTN_WIDTH + 2 * CONV_WIDTH + CONV_WIDTH + D_MODEL + D_MODEL

kernel_name = "hybrid_mla_conformer_gated_encoder"


def rms_norm(x, g):
    xf = x.astype(jnp.float32)
    y = xf * lax.rsqrt(jnp.mean(xf * xf, axis=-1, keepdims=True) + EPS) * g.astype(jnp.float32)
    return y.astype(x.dtype)


def layer_norm(x, g, b):
    xf = x.astype(jnp.float32)
    mu = jnp.mean(xf, axis=-1, keepdims=True)
    var = jnp.mean(jnp.square(xf - mu), axis=-1, keepdims=True)
    y = (xf - mu) * lax.rsqrt(var + EPS) * g.astype(jnp.float32) + b.astype(jnp.float32)
    return y.astype(x.dtype)


def split_columns(p):
    outs = []
    off = 0
    for n in SPLIT_SIZES:
        outs.append(p[..., off:off + n])
        off += n
    return outs


def rope_tables(seq_len, dtype):
    half = QK_ROPE_DIM // 2
    inv_freq = 1.0 / (ROPE_THETA ** (jnp.arange(half, dtype=jnp.float32) / half))
    ang = jnp.arange(seq_len, dtype=jnp.float32)[:, None] * inv_freq[None, :]
    return jnp.cos(ang).astype(dtype)[None, :, None, :], jnp.sin(ang).astype(dtype)[None, :, None, :]


def apply_rope(x, cos, sin):
    x1, x2 = x[..., :QK_ROPE_DIM // 2], x[..., QK_ROPE_DIM // 2:]
    return jnp.concatenate([x1 * cos - x2 * sin, x2 * cos + x1 * sin], axis=-1)


def blocked_attention(q, k, v):
    B, S, H, dh = q.shape
    nb = S // Q_BLOCK
    scale = 1.0 / math.sqrt(QK_HEAD_DIM)
    qb = jnp.transpose(q.reshape(B, nb, Q_BLOCK, H, dh), (1, 0, 2, 3, 4))

    def one_block(qi):
        s = jnp.einsum('bqhd,bkhd->bhqk', qi, k).astype(jnp.float32) * scale
        p = jax.nn.softmax(s, axis=-1).astype(v.dtype)
        return jnp.einsum('bhqk,bkhd->bqhd', p, v)

    o = lax.map(one_block, qb)
    return jnp.transpose(o, (1, 0, 2, 3, 4)).reshape(B, S, H * V_HEAD_DIM)


def hybrid_layer(x, norm_g, w_in, q_lora_g, w_uq, kv_lora_g, w_ukv, q_head_g, k_head_g, w_o_attn,
                 dw_kernel, dw_bias, conv_ln_g, conv_ln_b, w_pw2, w_out):
    B, S, _ = x.shape
    h = rms_norm(x, norm_g)
    proj = jnp.einsum('bsd,de->bse', h, w_in)
    c_q, c_kv, k_rope, gate_a, conv_in, gate_c, mg_a, mg_c = split_columns(proj)

    q = jnp.einsum('bsr,re->bse', rms_norm(c_q, q_lora_g), w_uq).reshape(B, S, N_HEADS, QK_HEAD_DIM)
    kv = jnp.einsum('bsr,re->bse', rms_norm(c_kv, kv_lora_g), w_ukv).reshape(B, S, N_HEADS, QK_NOPE_DIM + V_HEAD_DIM)
    k_nope, v = kv[..., :QK_NOPE_DIM], kv[..., QK_NOPE_DIM:]
    k_rope_h = jnp.broadcast_to(k_rope[:, :, None, :], (B, S, N_HEADS, QK_ROPE_DIM))
    k = jnp.concatenate([k_nope, k_rope_h], axis=-1)
    q = rms_norm(q, q_head_g)
    k = rms_norm(k, k_head_g)
    cos, sin = rope_tables(S, x.dtype)
    q = jnp.concatenate([q[..., :QK_NOPE_DIM], apply_rope(q[..., QK_NOPE_DIM:], cos, sin)], axis=-1)
    k = jnp.concatenate([k[..., :QK_NOPE_DIM], apply_rope(k[..., QK_NOPE_DIM:], cos, sin)], axis=-1)
    attn = blocked_attention(q, k, v)
    y_a = jnp.einsum('bse,ed->bsd', attn * jax.nn.silu(gate_a), w_o_attn)

    u = conv_in[..., :CONV_WIDTH] * jax.nn.sigmoid(conv_in[..., CONV_WIDTH:])
    kern = dw_kernel.astype(u.dtype)[:, None, :]
    u = lax.conv_general_dilated(u, kern, window_strides=(1,), padding=[(CONV_PAD, CONV_PAD)],
                                 dimension_numbers=('NWC', 'WIO', 'NWC'),
                                 feature_group_count=CONV_WIDTH) + dw_bias
    u = jax.nn.silu(layer_norm(u, conv_ln_g, conv_ln_b)) * jax.nn.silu(gate_c)
    y_c = jnp.einsum('bsc,cd->bsd', u, w_pw2)

    merged = jax.nn.sigmoid(mg_a) * y_a + jax.nn.sigmoid(mg_c) * y_c
    return x + jnp.einsum('bsd,de->bse', merged, w_out)


def setup_inputs(seed: int = 0) -> dict:
    key = jax.random.key(seed)
    ks = jax.random.split(key, 20)
    f32 = jnp.float32

    def w(k, shape, fan_in):
        return jax.random.normal(k, shape, f32) * (fan_in ** -0.5)

    def gain(k, shape):
        return 1.0 + 0.02 * jax.random.normal(k, shape, f32)

    return {
        "x_prompt": jax.random.normal(ks[0], (BATCH, SEQ, D_MODEL), f32),
        "x_sample": jax.random.normal(ks[1], (DEC_BATCH, DEC_SEQ, D_MODEL), f32),
        "norm_g": gain(ks[2], (DEPTH, D_MODEL)),
        "w_in": w(ks[3], (DEPTH, D_MODEL, D_IN), D_MODEL),
        "q_lora_g": gain(ks[4], (DEPTH, Q_LORA_RANK)),
        "w_uq": w(ks[5], (DEPTH, Q_LORA_RANK, N_HEADS * QK_HEAD_DIM), Q_LORA_RANK),
        "kv_lora_g": gain(ks[6], (DEPTH, KV_LORA_RANK)),
        "w_ukv": w(ks[7], (DEPTH, KV_LORA_RANK, N_HEADS * (QK_NOPE_DIM + V_HEAD_DIM)), KV_LORA_RANK),
        "q_head_g": gain(ks[8], (DEPTH, QK_HEAD_DIM)),
        "k_head_g": gain(ks[9], (DEPTH, QK_HEAD_DIM)),
        "w_o_attn": w(ks[10], (DEPTH, ATTN_WIDTH, D_MODEL), ATTN_WIDTH),
        "dw_kernel": w(ks[11], (DEPTH, CONV_KERNEL, CONV_WIDTH), CONV_KERNEL),
        "dw_bias": 0.02 * jax.random.normal(ks[12], (DEPTH, CONV_WIDTH), f32),
        "conv_ln_g": gain(ks[13], (DEPTH, CONV_WIDTH)),
        "conv_ln_b": 0.02 * jax.random.normal(ks[14], (DEPTH, CONV_WIDTH), f32),
        "w_pw2": w(ks[15], (DEPTH, CONV_WIDTH, D_MODEL), CONV_WIDTH),
        "w_out": w(ks[16], (DEPTH, D_MODEL, D_MODEL), D_MODEL),
    }


def reference(x_prompt, x_sample, norm_g, w_in, q_lora_g, w_uq, kv_lora_g, w_ukv, q_head_g, k_head_g,
              w_o_attn, dw_kernel, dw_bias, conv_ln_g, conv_ln_b, w_pw2, w_out):
    y_prompt = x_prompt
    y_sample = x_sample
    for l in range(DEPTH):
        layer_args = (norm_g[l], w_in[l], q_lora_g[l], w_uq[l], kv_lora_g[l], w_ukv[l], q_head_g[l], k_head_g[l],
                      w_o_attn[l], dw_kernel[l], dw_bias[l], conv_ln_g[l], conv_ln_b[l], w_pw2[l], w_out[l])
        y_prompt = hybrid_layer(y_prompt, *layer_args)
        y_sample = hybrid_layer(y_sample, *layer_args)
    return (y_prompt, y_sample)
```

```python
import functools
import math

import jax
import jax.numpy as jnp
from jax import lax
from jax.experimental import pallas as pl
from jax.experimental.pallas import tpu as pltpu

D_MODEL = 1024
N_HEADS = 8
NOPE = 128
ROPE = 64
QK_DIM = NOPE + ROPE
QK_PAD = 256
DV = 128
Q_RANK = 384
KV_RANK = 256
ATTN_WIDTH = N_HEADS * DV
CONV_W = D_MODEL
CONV_K = 31
CONV_PAD = (CONV_K - 1) // 2
ROPE_THETA = 10000.0
EPS = 1e-6
LANES = 128
HALO = 16
N_SLABS = CONV_W // LANES
SPLITS = (Q_RANK, KV_RANK, ROPE, ATTN_WIDTH, 2 * CONV_W, CONV_W, D_MODEL, D_MODEL)

F32 = jnp.float32
BF16 = jnp.bfloat16


def _const_spec(shape):
    return pl.BlockSpec(shape, lambda *_: (0,) * len(shape), pipeline_mode=pl.Buffered(1))


def _proj_body(x_ref, cs_ref, ng_ref, wcq_ref, wkv_ref, wa_ref, wb_ref, qlg_ref, wuq_ref,
               kvlg_ref, wukv_ref, qg_ref, kg_ref, q_ref, k_ref, v_ref, u_ref):
    x = x_ref[0]
    inv = lax.rsqrt(jnp.mean(x * x, axis=-1, keepdims=True) + EPS)
    h = (x * inv * ng_ref[...]).astype(BF16)
    cs = cs_ref[...]
    scale = 1.0 / math.sqrt(QK_DIM)

    cq = jnp.dot(h, wcq_ref[...], preferred_element_type=F32)
    cq_inv = lax.rsqrt(jnp.mean(cq * cq, axis=-1, keepdims=True) + EPS)
    cqn = (cq * cq_inv * qlg_ref[...]).astype(BF16)
    qa = jnp.dot(cqn, wuq_ref[...], preferred_element_type=F32)
    q_cs = cs * qg_ref[1:2, :]
    for hh in range(N_HEADS):
        nope = qa[:, LANES * hh:LANES * (hh + 1)]
        rt = qa[:, ATTN_WIDTH + LANES * hh:ATTN_WIDTH + LANES * (hh + 1)]
        ss = jnp.sum(nope * nope + 0.5 * (rt * rt), axis=-1, keepdims=True)
        inv_h = lax.rsqrt(ss * (1.0 / QK_DIM) + EPS) * scale
        q_ref[0, hh, :, 0:LANES] = (nope * inv_h * qg_ref[0:1, :]).astype(BF16)
        t = rt * q_cs
        r = t + pltpu.roll(t, ROPE, 1)
        q_ref[0, hh, :, LANES:2 * LANES] = (r * inv_h).astype(BF16)

    kvin = jnp.dot(h, wkv_ref[...], preferred_element_type=F32)
    ckv = kvin[:, :KV_RANK]
    krt = kvin[:, KV_RANK:KV_RANK + LANES]
    ckv_inv = lax.rsqrt(jnp.mean(ckv * ckv, axis=-1, keepdims=True) + EPS)
    ckvn = (ckv * ckv_inv * kvlg_ref[...]).astype(BF16)
    kva = jnp.dot(ckvn, wukv_ref[...], preferred_element_type=F32)
    kr_sq = 0.5 * (krt * krt)
    tk = krt * (cs * kg_ref[1:2, :])
    rk = tk + pltpu.roll(tk, ROPE, 1)
    lane = lax.broadcasted_iota(jnp.int32, rk.shape, 1)
    rk = jnp.where(lane < ROPE, rk, 0.0)
    for hh in range(N_HEADS):
        kn = kva[:, LANES * hh:LANES * (hh + 1)]
        ss = jnp.sum(kn * kn + kr_sq, axis=-1, keepdims=True)
        inv_h = lax.rsqrt(ss * (1.0 / QK_DIM) + EPS)
        k_ref[0, hh, :, 0:LANES] = (kn * inv_h * kg_ref[0:1, :]).astype(BF16)
        k_ref[0, hh, :, LANES:2 * LANES] = (rk * inv_h).astype(BF16)
        v_ref[0, hh] = kva[:, ATTN_WIDTH + LANES * hh:ATTN_WIDTH + LANES * (hh + 1)].astype(BF16)

    a = jnp.dot(h, wa_ref[...], preferred_element_type=F32)
    b = jnp.dot(h, wb_ref[...], preferred_element_type=F32)
    u_ref[0] = (a * jax.nn.sigmoid(b)).astype(BF16)


def _proj_call(x, cs, w, tm):
    B, S, D = x.shape
    grid = (B, S // tm)
    in_specs = [
        pl.BlockSpec((1, tm, D), lambda b, i: (b, i, 0)),
        pl.BlockSpec((tm, LANES), lambda b, i: (i, 0)),
        _const_spec((1, D)),
        _const_spec((D, Q_RANK)),
        _const_spec((D, KV_RANK + LANES)),
        _const_spec((D, CONV_W)),
        _const_spec((D, CONV_W)),
        _const_spec((1, Q_RANK)),
        _const_spec((Q_RANK, 2 * ATTN_WIDTH)),
        _const_spec((1, KV_RANK)),
        _const_spec((KV_RANK, 2 * ATTN_WIDTH)),
        _const_spec((2, LANES)),
        _const_spec((2, LANES)),
    ]
    out_specs = [
        pl.BlockSpec((1, N_HEADS, tm, QK_PAD), lambda b, i: (b, 0, i, 0)),
        pl.BlockSpec((1, N_HEADS, tm, QK_PAD), lambda b, i: (b, 0, i, 0)),
        pl.BlockSpec((1, N_HEADS, tm, DV), lambda b, i: (b, 0, i, 0)),
        pl.BlockSpec((1, tm, CONV_W), lambda b, i: (b, i, 0)),
    ]
    out_shape = [
        jax.ShapeDtypeStruct((B, N_HEADS, S, QK_PAD), BF16),
        jax.ShapeDtypeStruct((B, N_HEADS, S, QK_PAD), BF16),
        jax.ShapeDtypeStruct((B, N_HEADS, S, DV), BF16),
        jax.ShapeDtypeStruct((B, S, CONV_W), BF16),
    ]
    return pl.pallas_call(
        _proj_body,
        grid=grid,
        in_specs=in_specs,
        out_specs=out_specs,
        out_shape=out_shape,
        compiler_params=pltpu.CompilerParams(
            dimension_semantics=("parallel", "parallel"),
            vmem_limit_bytes=56 * 1024 * 1024),
        name="mla_conv_proj",
    )(x, cs, w["norm_g"], w["w_cq"], w["w_kvin"], w["w_a"], w["w_b"], w["q_lora_g"], w["w_uq"],
      w["kv_lora_g"], w["w_ukv"], w["q_head_g"], w["k_head_g"])


def _attn_body(q_ref, k_ref, v_ref, o_ref, vaug_ref, *, tq):
    S = k_ref.shape[2]
    vaug_ref[:, 0:DV] = v_ref[0, 0]
    vaug_ref[:, DV:2 * DV] = jnp.ones((S, DV), BF16)
    k = k_ref[0, 0]
    for i in range(S // tq):
        q = q_ref[0, 0, i * tq:(i + 1) * tq, :]
        s = lax.dot_general(q, k, (((1,), (1,)), ((), ())), preferred_element_type=F32)
        m = jnp.max(s, axis=-1, keepdims=True)
        p = jnp.exp(s - m).astype(BF16)
        o = jnp.dot(p, vaug_ref[...], preferred_element_type=F32)
        o_ref[0, i * tq:(i + 1) * tq, :] = (o[:, 0:DV] / o[:, DV:2 * DV]).astype(BF16)


def _attn_call(q, k, v, tq):
    B, H, S, _ = q.shape
    return pl.pallas_call(
        functools.partial(_attn_body, tq=tq),
        grid=(B, H),
        in_specs=[
            pl.BlockSpec((1, 1, S, QK_PAD), lambda b, h: (b, h, 0, 0)),
            pl.BlockSpec((1, 1, S, QK_PAD), lambda b, h: (b, h, 0, 0)),
            pl.BlockSpec((1, 1, S, DV), lambda b, h: (b, h, 0, 0)),
        ],
        out_specs=pl.BlockSpec((1, S, DV), lambda b, h: (b, 0, h)),
        out_shape=jax.ShapeDtypeStruct((B, S, H * DV), BF16),
        scratch_shapes=[pltpu.VMEM((S, 2 * DV), BF16)],
        compiler_params=pltpu.CompilerParams(
            dimension_semantics=("parallel", "parallel"),
            vmem_limit_bytes=56 * 1024 * 1024),
        name="mla_attention",
    )(q, k, v)


def _merge_body(x_ref, attn_ref, u_ref, up_ref, un_ref, ng_ref, wg_ref, wo_ref, wpw_ref, wout_ref,
                dw_ref, dwb_ref, lng_ref, lnb_ref, o_ref, ubuf_ref, cbuf_ref, *, tm, rows):
    i = pl.program_id(1)
    n = pl.num_programs(1)
    x = x_ref[0]
    inv = lax.rsqrt(jnp.mean(x * x, axis=-1, keepdims=True) + EPS)
    h = (x * inv * ng_ref[...]).astype(BF16)

    has_prev = i > 0
    has_next = i < n - 1
    for j in range(N_SLABS):
        sl = slice(LANES * j, LANES * (j + 1))
        ubuf_ref[j, 0:HALO, :] = jnp.where(has_prev, up_ref[0, :, sl].astype(F32), 0.0)
        ubuf_ref[j, HALO:HALO + tm, :] = u_ref[0, :, sl].astype(F32)
        ubuf_ref[j, HALO + tm:2 * HALO + tm, :] = jnp.where(has_next, un_ref[0, :, sl].astype(F32), 0.0)

    def conv_chunk(c, carry):
        r0 = pl.multiple_of(c * rows, rows)
        for j in range(N_SLABS):
            sl = slice(LANES * j, LANES * (j + 1))
            acc = jnp.zeros((rows, LANES), F32)
            for kk in range(CONV_K):
                win = ubuf_ref[j, pl.ds(r0 + (HALO - CONV_PAD) + kk, rows), :]
                acc = acc + dw_ref[kk:kk + 1, sl] * win
            cbuf_ref[pl.ds(r0, rows), sl] = acc + dwb_ref[:, sl]
        return carry

    lax.fori_loop(0, tm // rows, conv_chunk, 0)

    ga = jax.nn.silu(jnp.dot(h, wg_ref[:, 0:D_MODEL], preferred_element_type=F32))
    ya_in = (attn_ref[0].astype(F32) * ga).astype(BF16)
    y_a = jnp.dot(ya_in, wo_ref[...], preferred_element_type=F32)

    c = cbuf_ref[...]
    mu = jnp.mean(c, axis=-1, keepdims=True)
    cc = c - mu
    var = jnp.mean(cc * cc, axis=-1, keepdims=True)
    ln = cc * lax.rsqrt(var + EPS) * lng_ref[...] + lnb_ref[...]
    gc = jax.nn.silu(jnp.dot(h, wg_ref[:, D_MODEL:2 * D_MODEL], preferred_element_type=F32))
    yc_in = (jax.nn.silu(ln) * gc).astype(BF16)
    y_c = jnp.dot(yc_in, wpw_ref[...], preferred_element_type=F32)

    ma = jax.nn.sigmoid(jnp.dot(h, wg_ref[:, 2 * D_MODEL:3 * D_MODEL], preferred_element_type=F32))
    mc = jax.nn.sigmoid(jnp.dot(h, wg_ref[:, 3 * D_MODEL:4 * D_MODEL], preferred_element_type=F32))
    merged = (ma * y_a + mc * y_c).astype(BF16)
    o_ref[0] = x + jnp.dot(merged, wout_ref[...], preferred_element_type=F32)


def _merge_call(x, attn, u, w, tm, rows):
    B, S, D = x.shape
    nt = S // tm
    hb = tm // HALO
    last_hb = S // HALO - 1
    in_specs = [
        pl.BlockSpec((1, tm, D), lambda b, i: (b, i, 0)),
        pl.BlockSpec((1, tm, ATTN_WIDTH), lambda b, i: (b, i, 0)),
        pl.BlockSpec((1, tm, CONV_W), lambda b, i: (b, i, 0)),
        pl.BlockSpec((1, HALO, CONV_W), lambda b, i: (b, jnp.maximum(i * hb - 1, 0), 0)),
        pl.BlockSpec((1, HALO, CONV_W), lambda b, i: (b, jnp.minimum((i + 1) * hb, last_hb), 0)),
        _const_spec((1, D)),
        _const_spec((D, 4 * D_MODEL)),
        _const_spec((ATTN_WIDTH, D_MODEL)),
        _const_spec((CONV_W, D_MODEL)),
        _const_spec((D_MODEL, D_MODEL)),
        _const_spec((CONV_K, CONV_W)),
        _const_spec((1, CONV_W)),
        _const_spec((1, CONV_W)),
        _const_spec((1, CONV_W)),
    ]
    return pl.pallas_call(
        functools.partial(_merge_body, tm=tm, rows=rows),
        grid=(B, nt),
        in_specs=in_specs,
        out_specs=pl.BlockSpec((1, tm, D), lambda b, i: (b, i, 0)),
        out_shape=jax.ShapeDtypeStruct((B, S, D), F32),
        scratch_shapes=[
            pltpu.VMEM((N_SLABS, tm + 2 * HALO, LANES), F32),
            pltpu.VMEM((tm, CONV_W), F32),
        ],
        compiler_params=pltpu.CompilerParams(
            dimension_semantics=("parallel", "parallel"),
            vmem_limit_bytes=56 * 1024 * 1024),
        name="conv_gate_merge",
    )(x, attn, u, u, u, w["norm_g"], w["w_gates"], w["w_o_attn"], w["w_pw2"], w["w_out"],
      w["dw_kernel"], w["dw_bias"], w["conv_ln_g"], w["conv_ln_b"])


def _half_swap(t):
    half = t.shape[-1] // 2
    return jnp.concatenate([-t[..., half:], t[..., :half]], axis=-1)


def _half_swap_unsigned(t):
    half = t.shape[-1] // 2
    return jnp.concatenate([t[..., half:], t[..., :half]], axis=-1)


def _prep_weights(norm_g, w_in, q_lora_g, w_uq, kv_lora_g, w_ukv, q_head_g, k_head_g, w_o_attn,
                  dw_kernel, dw_bias, conv_ln_g, conv_ln_b, w_pw2, w_out):
    offs = [0]
    for nsz in SPLITS:
        offs.append(offs[-1] + nsz)
    col = lambda g: w_in[:, offs[g]:offs[g + 1]]
    w_kr = col(2)
    w_conv = col(4)
    uq = w_uq.reshape(Q_RANK, N_HEADS, QK_DIM)
    uq_rope = uq[:, :, NOPE:]
    uq_tiles = jnp.concatenate([uq_rope, _half_swap(uq_rope)], axis=-1)
    ukv = w_ukv.reshape(KV_RANK, N_HEADS, NOPE + DV)

    def head_gain(g):
        g_rope = g[NOPE:]
        return jnp.stack([g[:NOPE], jnp.concatenate([g_rope, _half_swap_unsigned(g_rope)])])

    return {
        "norm_g": norm_g.reshape(1, D_MODEL),
        "w_cq": col(0).astype(BF16),
        "w_kvin": jnp.concatenate([col(1), w_kr, _half_swap(w_kr)], axis=1).astype(BF16),
        "w_a": w_conv[:, :CONV_W].astype(BF16),
        "w_b": w_conv[:, CONV_W:].astype(BF16),
        "q_lora_g": q_lora_g.reshape(1, Q_RANK),
        "w_uq": jnp.concatenate([uq[:, :, :NOPE].reshape(Q_RANK, ATTN_WIDTH),
                                 uq_tiles.reshape(Q_RANK, ATTN_WIDTH)], axis=1).astype(BF16),
        "kv_lora_g": kv_lora_g.reshape(1, KV_RANK),
        "w_ukv": jnp.concatenate([ukv[:, :, :NOPE].reshape(KV_RANK, ATTN_WIDTH),
                                  ukv[:, :, NOPE:].reshape(KV_RANK, ATTN_WIDTH)], axis=1).astype(BF16),
        "q_head_g": head_gain(q_head_g),
        "k_head_g": head_gain(k_head_g),
        "w_gates": jnp.concatenate([col(3), col(5), col(6), col(7)], axis=1).astype(BF16),
        "w_o_attn": w_o_attn.astype(BF16),
        "w_pw2": w_pw2.astype(BF16),
        "w_out": w_out.astype(BF16),
        "dw_kernel": dw_kernel,
        "dw_bias": dw_bias.reshape(1, CONV_W),
        "conv_ln_g": conv_ln_g.reshape(1, CONV_W),
        "conv_ln_b": conv_ln_b.reshape(1, CONV_W),
    }


def _rope_table(seq_len):
    half = ROPE // 2
    inv_freq = 1.0 / (ROPE_THETA ** (jnp.arange(half, dtype=F32) / half))
    ang = jnp.arange(seq_len, dtype=F32)[:, None] * inv_freq[None, :]
    cos, sin = jnp.cos(ang), jnp.sin(ang)
    return jnp.concatenate([cos, cos, sin, sin], axis=1)


def _layer(x, w, *, tm_proj=512, tq=512, tm_merge=512, conv_rows=64):
    S = x.shape[1]
    cs = _rope_table(S)
    q, k, v, u = _proj_call(x, cs, w, tm_proj)
    attn = _attn_call(q, k, v, tq)
    return _merge_call(x, attn, u, w, tm_merge, conv_rows)


def kernel(x_prompt, x_sample, norm_g, w_in, q_lora_g, w_uq, kv_lora_g, w_ukv, q_head_g, k_head_g,
           w_o_attn, dw_kernel, dw_bias, conv_ln_g, conv_ln_b, w_pw2, w_out):
    depth = norm_g.shape[0]
    y_prompt, y_sample = x_prompt, x_sample
    for l in range(depth):
        w = _prep_weights(norm_g[l], w_in[l], q_lora_g[l], w_uq[l], kv_lora_g[l], w_ukv[l],
                          q_head_g[l], k_head_g[l], w_o_attn[l], dw_kernel[l], dw_bias[l],
                          conv_ln_g[l], conv_ln_b[l], w_pw2[l], w_out[l])
        y_prompt = _layer(y_prompt, w)
        y_sample = _layer(y_sample, w)
    return (y_prompt, y_sample)
```

```python
import functools
import math

import jax
import jax.numpy as jnp
from jax import lax
from jax.experimental import pallas as pl
from jax.experimental.pallas import tpu as pltpu

D_MODEL = 1024
N_HEADS = 8
NOPE = 128
ROPE = 64
QK_DIM = NOPE + ROPE
QK_PAD = 256
DV = 128
Q_RANK = 384
KV_RANK = 256
ATTN_WIDTH = N_HEADS * DV
CONV_W = D_MODEL
CONV_K = 31
CONV_PAD = (CONV_K - 1) // 2
ROPE_THETA = 10000.0
EPS = 1e-6
LANES = 128
HALO = 16
SPLITS = (Q_RANK, KV_RANK, ROPE, ATTN_WIDTH, 2 * CONV_W, CONV_W, D_MODEL, D_MODEL)

VMEM_LIMIT_BYTES = 56 * 1024 * 1024
TM_PROJ = 512
TM_MERGE = 512
TQ = 256
CONV_ROWS = 64

F32 = jnp.float32
BF16 = jnp.bfloat16

assert N_HEADS * LANES == CONV_W


def _const_spec(shape):
    return pl.BlockSpec(shape, lambda *_: (0,) * len(shape), pipeline_mode=pl.Buffered(1))


def _params(**flags):
    return pltpu.CompilerParams(dimension_semantics=("parallel", "parallel"),
                                vmem_limit_bytes=VMEM_LIMIT_BYTES, flags=flags or None)


def _proj_body(x_ref, cs_ref, cst_ref, ng_ref, wcq_ref, wkv_ref, wa_ref, wb_ref, qlg_ref, wuq_ref,
               kvlg_ref, wukt_ref, wuv_ref, qg_ref, kgt_ref, q_ref, k_ref, v_ref, u_ref):
    tm = x_ref.shape[1]
    x = x_ref[0]
    inv = lax.rsqrt(jnp.mean(x * x, axis=-1, keepdims=True) + EPS)
    h = (x * inv * ng_ref[...]).astype(BF16)
    scale = math.log2(math.e) / math.sqrt(QK_DIM)

    cq = jnp.dot(h, wcq_ref[...], preferred_element_type=F32)
    cq_inv = lax.rsqrt(jnp.mean(cq * cq, axis=-1, keepdims=True) + EPS)
    cqn = (cq * cq_inv * qlg_ref[...]).astype(BF16)
    qa = jnp.dot(cqn, wuq_ref[...], preferred_element_type=F32)
    q_cs = cs_ref[...] * qg_ref[1:2, :]
    for hh in range(N_HEADS):
        nope = qa[:, LANES * hh:LANES * (hh + 1)]
        rt = qa[:, ATTN_WIDTH + LANES * hh:ATTN_WIDTH + LANES * (hh + 1)]
        ss = jnp.sum(nope * nope + 0.5 * (rt * rt), axis=-1, keepdims=True)
        inv_h = lax.rsqrt(ss * (1.0 / QK_DIM) + EPS) * scale
        q_ref[0, hh, :, 0:LANES] = (nope * inv_h * qg_ref[0:1, :]).astype(BF16)
        t = rt * q_cs
        r = t + pltpu.roll(t, ROPE, 1)
        q_ref[0, hh, :, LANES:2 * LANES] = (r * inv_h).astype(BF16)

    kvin = jnp.dot(h, wkv_ref[...], preferred_element_type=F32)
    ckv = kvin[:, :KV_RANK]
    ckv_inv = lax.rsqrt(jnp.mean(ckv * ckv, axis=-1, keepdims=True) + EPS)
    ckvn = ckv * ckv_inv * kvlg_ref[...]
    v_all = jnp.dot(ckvn.astype(BF16), wuv_ref[...], preferred_element_type=F32)
    for hh in range(N_HEADS):
        v_ref[0, hh] = v_all[:, DV * hh:DV * (hh + 1)].astype(BF16)

    knt = jnp.dot(wukt_ref[...], ckvn.T.astype(BF16), preferred_element_type=F32)
    krt = kvin[:, KV_RANK:KV_RANK + LANES].T
    kr_ss = 0.5 * jnp.sum(krt * krt, axis=0, keepdims=True)
    g_nope = jnp.broadcast_to(kgt_ref[:, 0:1], (LANES, tm))
    g_rope = jnp.broadcast_to(kgt_ref[:, 1:2], (LANES, tm))
    t = krt * g_rope * cst_ref[...]
    rk = t[0:ROPE] + t[ROPE:2 * ROPE]
    pad = jnp.zeros((QK_PAD - QK_DIM, tm), BF16)
    for hh in range(N_HEADS):
        kn = knt[NOPE * hh:NOPE * (hh + 1), :]
        ss = jnp.sum(kn * kn, axis=0, keepdims=True) + kr_ss
        inv_h = lax.rsqrt(ss * (1.0 / QK_DIM) + EPS)
        k_ref[0, hh, 0:NOPE, :] = (kn * inv_h * g_nope).astype(BF16)
        k_ref[0, hh, NOPE:QK_DIM, :] = (rk * inv_h).astype(BF16)
        k_ref[0, hh, QK_DIM:QK_PAD, :] = pad

    a = jnp.dot(h, wa_ref[...], preferred_element_type=F32)
    b = jnp.dot(h, wb_ref[...], preferred_element_type=F32)
    u_ref[0] = (a * jax.nn.sigmoid(b)).astype(BF16)


def _proj_call(x, cs, cst, w):
    B, S, D = x.shape
    tm = TM_PROJ
    in_specs = [
        pl.BlockSpec((1, tm, D), lambda b, i: (b, i, 0)),
        pl.BlockSpec((tm, LANES), lambda b, i: (i, 0)),
        pl.BlockSpec((LANES, tm), lambda b, i: (0, i)),
        _const_spec((1, D)),
        _const_spec((D, Q_RANK)),
        _const_spec((D, KV_RANK + LANES)),
        _const_spec((D, CONV_W)),
        _const_spec((D, CONV_W)),
        _const_spec((1, Q_RANK)),
        _const_spec((Q_RANK, 2 * ATTN_WIDTH)),
        _const_spec((1, KV_RANK)),
        _const_spec((N_HEADS * NOPE, KV_RANK)),
        _const_spec((KV_RANK, ATTN_WIDTH)),
        _const_spec((2, LANES)),
        _const_spec((LANES, 2)),
    ]
    out_specs = [
        pl.BlockSpec((1, N_HEADS, tm, QK_PAD), lambda b, i: (b, 0, i, 0)),
        pl.BlockSpec((1, N_HEADS, QK_PAD, tm), lambda b, i: (b, 0, 0, i)),
        pl.BlockSpec((1, N_HEADS, tm, DV), lambda b, i: (b, 0, i, 0)),
        pl.BlockSpec((1, tm, CONV_W), lambda b, i: (b, i, 0)),
    ]
    out_shape = [
        jax.ShapeDtypeStruct((B, N_HEADS, S, QK_PAD), BF16),
        jax.ShapeDtypeStruct((B, N_HEADS, QK_PAD, S), BF16),
        jax.ShapeDtypeStruct((B, N_HEADS, S, DV), BF16),
        jax.ShapeDtypeStruct((B, S, CONV_W), BF16),
    ]
    return pl.pallas_call(
        _proj_body,
        grid=(B, S // tm),
        in_specs=in_specs,
        out_specs=out_specs,
        out_shape=out_shape,
        compiler_params=_params(),
        name="mla_conv_proj",
    )(x, cs, cst, w["norm_g"], w["w_cq"], w["w_kvin"], w["w_a"], w["w_b"], w["q_lora_g"], w["w_uq"],
      w["kv_lora_g"], w["w_ukt"], w["w_uv"], w["q_head_g"], w["k_head_gt"])


def _attn_body(q_ref, k_ref, v_ref, u_ref, dw_ref, dwb_ref, o_ref, c_ref,
               vaug_ref, s_ref, p_ref, ubuf_ref):
    S = k_ref.shape[3]
    tq = s_ref.shape[2]
    vaug_ref[:, 0:DV] = v_ref[0, 0]
    vaug_ref[:, DV:2 * DV] = jnp.ones((S, DV), BF16)

    n_rounds = S // (2 * tq)

    ubuf_ref[0:HALO, :] = jnp.zeros((HALO, LANES), F32)
    ubuf_ref[HALO:HALO + S, :] = u_ref[0].astype(F32)
    ubuf_ref[HALO + S:2 * HALO + S, :] = jnp.zeros((HALO, LANES), F32)

    def conv_chunk(c):
        r0 = c * CONV_ROWS
        acc = jnp.zeros((CONV_ROWS, LANES), F32)
        for kk in range(CONV_K):
            start = r0 + (HALO - CONV_PAD) + kk
            acc = acc + dw_ref[kk:kk + 1, :] * ubuf_ref[start:start + CONV_ROWS, :]
        out = acc + dwb_ref[...]
        c_ref[0, r0:r0 + CONV_ROWS, :] = out
        bits = lax.bitcast_convert_type(out[0:8, :], jnp.uint32)
        return (bits >> 16) >> 16

    n_chunks = S // CONV_ROWS
    chunks_per_block = n_chunks // (2 * n_rounds)
    assert chunks_per_block * 2 * n_rounds == n_chunks

    def rows(r, j):
        return slice((2 * r + j) * tq, (2 * r + j + 1) * tq)

    def qk(r):
        for j in range(2):
            s_ref[r % 2, j] = jnp.dot(q_ref[0, 0, rows(r, j), :], k_ref[0, 0],
                                      preferred_element_type=F32)

    def softmax(r):
        for j in range(2):
            first = (2 * r + j) * chunks_per_block
            zero = conv_chunk(first)
            for c in range(first + 1, first + chunks_per_block):
                zero = zero | conv_chunk(c)
            head = lax.bitcast_convert_type(s_ref[r % 2, j, 0:8, 0:LANES], jnp.uint32)
            s_ref[r % 2, j, 0:8, 0:LANES] = lax.bitcast_convert_type(head | zero, F32)
            s = s_ref[r % 2, j]
            p_ref[r % 2, j] = jnp.exp2(s - jnp.max(s, axis=-1, keepdims=True)).astype(BF16)

    def pv(r):
        os_ = [jnp.dot(p_ref[r % 2, j], vaug_ref[...], preferred_element_type=F32) for j in range(2)]
        for j in range(2):
            o = os_[j]
            o_ref[0, rows(r, j), :] = (o[:, 0:DV] / o[:, DV:2 * DV]).astype(BF16)

    qk(0)
    for r in range(n_rounds):
        if r + 1 < n_rounds:
            qk(r + 1)
        softmax(r)
        pv(r)


def _attn_call(q, kt, v, u, w):
    B, H, S, _ = q.shape
    slab = lambda b, h: (b, 0, h)
    return pl.pallas_call(
        _attn_body,
        grid=(B, H),
        in_specs=[
            pl.BlockSpec((1, 1, S, QK_PAD), lambda b, h: (b, h, 0, 0)),
            pl.BlockSpec((1, 1, QK_PAD, S), lambda b, h: (b, h, 0, 0)),
            pl.BlockSpec((1, 1, S, DV), lambda b, h: (b, h, 0, 0)),
            pl.BlockSpec((1, S, LANES), slab),
            pl.BlockSpec((CONV_K, LANES), lambda b, h: (0, h)),
            pl.BlockSpec((1, LANES), lambda b, h: (0, h)),
        ],
        out_specs=[pl.BlockSpec((1, S, DV), slab), pl.BlockSpec((1, S, LANES), slab)],
        out_shape=[jax.ShapeDtypeStruct((B, S, H * DV), BF16),
                   jax.ShapeDtypeStruct((B, S, CONV_W), F32)],
        scratch_shapes=[pltpu.VMEM((S, 2 * DV), BF16),
                        pltpu.VMEM((2, 2, TQ, S), F32),
                        pltpu.VMEM((2, 2, TQ, S), BF16),
                        pltpu.VMEM((S + 2 * HALO, LANES), F32)],
        compiler_params=_params(),
        name="mla_attention_conv",
    )(q, kt, v, u, w["dw_kernel"], w["dw_bias"])


def _merge_body(x_ref, attn_ref, c_ref, ng_ref, wg_ref, wo_ref, wpw_ref, wout_ref,
                lng_ref, lnb_ref, o_ref):
    x = x_ref[0]
    inv = lax.rsqrt(jnp.mean(x * x, axis=-1, keepdims=True) + EPS)
    h = (x * inv * ng_ref[...]).astype(BF16)

    ga = jax.nn.silu(jnp.dot(h, wg_ref[:, 0:D_MODEL], preferred_element_type=F32))
    ya_in = (attn_ref[0].astype(F32) * ga).astype(BF16)
    y_a = jnp.dot(ya_in, wo_ref[...], preferred_element_type=F32)

    c = c_ref[0]
    mu = jnp.mean(c, axis=-1, keepdims=True)
    cc = c - mu
    var = jnp.mean(cc * cc, axis=-1, keepdims=True)
    ln = cc * lax.rsqrt(var + EPS) * lng_ref[...] + lnb_ref[...]
    gc = jax.nn.silu(jnp.dot(h, wg_ref[:, D_MODEL:2 * D_MODEL], preferred_element_type=F32))
    yc_in = (jax.nn.silu(ln) * gc).astype(BF16)
    y_c = jnp.dot(yc_in, wpw_ref[...], preferred_element_type=F32)

    ma = jax.nn.sigmoid(jnp.dot(h, wg_ref[:, 2 * D_MODEL:3 * D_MODEL], preferred_element_type=F32))
    mc = jax.nn.sigmoid(jnp.dot(h, wg_ref[:, 3 * D_MODEL:4 * D_MODEL], preferred_element_type=F32))
    merged = (ma * y_a + mc * y_c).astype(BF16)
    o_ref[0] = x + jnp.dot(merged, wout_ref[...], preferred_element_type=F32)


def _merge_call(x, attn, c, w):
    B, S, D = x.shape
    tm = TM_MERGE
    tile = lambda width: pl.BlockSpec((1, tm, width), lambda b, i: (b, i, 0))
    in_specs = [
        tile(D), tile(ATTN_WIDTH), tile(CONV_W),
        _const_spec((1, D)),
        _const_spec((D, 4 * D_MODEL)),
        _const_spec((ATTN_WIDTH, D_MODEL)),
        _const_spec((CONV_W, D_MODEL)),
        _const_spec((D_MODEL, D_MODEL)),
        _const_spec((1, CONV_W)),
        _const_spec((1, CONV_W)),
    ]
    return pl.pallas_call(
        _merge_body,
        grid=(B, S // tm),
        in_specs=in_specs,
        out_specs=tile(D),
        out_shape=jax.ShapeDtypeStruct((B, S, D), F32),
        compiler_params=_params(),
        name="gate_merge",
    )(x, attn, c, w["norm_g"], w["w_gates"], w["w_o_attn"], w["w_pw2"], w["w_out"],
      w["conv_ln_g"], w["conv_ln_b"])


def _half_swap(t, sign):
    half = t.shape[-1] // 2
    return jnp.concatenate([sign * t[..., half:], t[..., :half]], axis=-1)


def _prep_weights(norm_g, w_in, q_lora_g, w_uq, kv_lora_g, w_ukv, q_head_g, k_head_g, w_o_attn,
                  dw_kernel, dw_bias, conv_ln_g, conv_ln_b, w_pw2, w_out):
    offs = [0]
    for nsz in SPLITS:
        offs.append(offs[-1] + nsz)
    col = lambda g: w_in[:, offs[g]:offs[g + 1]]
    w_kr = col(2)
    w_conv = col(4)
    uq = w_uq.reshape(Q_RANK, N_HEADS, QK_DIM)
    uq_rope = uq[:, :, NOPE:]
    uq_tiles = jnp.concatenate([uq_rope, _half_swap(uq_rope, -1.0)], axis=-1)
    ukv = w_ukv.reshape(KV_RANK, N_HEADS, NOPE + DV)

    def head_gain(g):
        g_rope = g[NOPE:]
        return jnp.stack([g[:NOPE], jnp.concatenate([g_rope, _half_swap(g_rope, 1.0)])])

    return {
        "norm_g": norm_g.reshape(1, D_MODEL),
        "w_cq": col(0).astype(BF16),
        "w_kvin": jnp.concatenate([col(1), w_kr, _half_swap(w_kr, -1.0)], axis=1).astype(BF16),
        "w_a": w_conv[:, :CONV_W].astype(BF16),
        "w_b": w_conv[:, CONV_W:].astype(BF16),
        "q_lora_g": q_lora_g.reshape(1, Q_RANK),
        "w_uq": jnp.concatenate([uq[:, :, :NOPE].reshape(Q_RANK, ATTN_WIDTH),
                                 uq_tiles.reshape(Q_RANK, ATTN_WIDTH)], axis=1).astype(BF16),
        "kv_lora_g": kv_lora_g.reshape(1, KV_RANK),
        "w_ukt": ukv[:, :, :NOPE].reshape(KV_RANK, N_HEADS * NOPE).T.astype(BF16),
        "w_uv": ukv[:, :, NOPE:].reshape(KV_RANK, ATTN_WIDTH).astype(BF16),
        "q_head_g": head_gain(q_head_g),
        "k_head_gt": head_gain(k_head_g).T,
        "w_gates": jnp.concatenate([col(3), col(5), col(6), col(7)], axis=1).astype(BF16),
        "w_o_attn": w_o_attn.astype(BF16),
        "w_pw2": w_pw2.astype(BF16),
        "w_out": w_out.astype(BF16),
        "dw_kernel": dw_kernel,
        "dw_bias": dw_bias.reshape(1, CONV_W),
        "conv_ln_g": conv_ln_g.reshape(1, CONV_W),
        "conv_ln_b": conv_ln_b.reshape(1, CONV_W),
    }


def _rope_table(seq_len):
    half = ROPE // 2
    inv_freq = 1.0 / (ROPE_THETA ** (jnp.arange(half, dtype=F32) / half))
    ang = jnp.arange(seq_len, dtype=F32)[:, None] * inv_freq[None, :]
    cos, sin = jnp.cos(ang), jnp.sin(ang)
    return jnp.concatenate([cos, cos, sin, sin], axis=1)


def _layer(x, w):
    cs = _rope_table(x.shape[1])
    q, kt, v, u = _proj_call(x, cs, cs.T, w)
    attn, c = _attn_call(q, kt, v, u, w)
    return _merge_call(x, attn, c, w)


def kernel(x_prompt, x_sample, norm_g, w_in, q_lora_g, w_uq, kv_lora_g, w_ukv, q_head_g, k_head_g,
           w_o_attn, dw_kernel, dw_bias, conv_ln_g, conv_ln_b, w_pw2, w_out):
    depth = norm_g.shape[0]
    y_prompt, y_sample = x_prompt, x_sample
    for l in range(depth):
        w = _prep_weights(norm_g[l], w_in[l], q_lora_g[l], w_uq[l], kv_lora_g[l], w_ukv[l],
                          q_head_g[l], k_head_g[l], w_o_attn[l], dw_kernel[l], dw_bias[l],
                          conv_ln_g[l], conv_ln_b[l], w_pw2[l], w_out[l])
        y_prompt = _layer(y_prompt, w)
        y_sample = _layer(y_sample, w)
    return (y_prompt, y_sample)
```

```python
import math

import jax
import jax.numpy as jnp
from jax import lax
from jax.experimental import pallas as pl
from jax.experimental.pallas import tpu as pltpu

D_MODEL = 1024
N_HEADS = 8
NOPE = 128
ROPE = 64
QK_DIM = NOPE + ROPE
QK_PAD = 256
DV = 128
Q_RANK = 384
KV_RANK = 256
ATTN_WIDTH = N_HEADS * DV
CONV_W = D_MODEL
CONV_K = 31
CONV_PAD = (CONV_K - 1) // 2
ROPE_THETA = 10000.0
EPS = 1e-6
LANES = 128
HALO = 16
SPLITS = (Q_RANK, KV_RANK, ROPE, ATTN_WIDTH, 2 * CONV_W, CONV_W, D_MODEL, D_MODEL)

VMEM_LIMIT_BYTES = 56 * 1024 * 1024
TM_PROJ = 512
TM_MERGE = 512
TQ = 256
CONV_ROWS = 32

F32 = jnp.float32
BF16 = jnp.bfloat16

assert N_HEADS * LANES == CONV_W


def _const_spec(shape):
    return pl.BlockSpec(shape, lambda *_: (0,) * len(shape), pipeline_mode=pl.Buffered(1))


def _params():
    return pltpu.CompilerParams(dimension_semantics=("parallel", "parallel"),
                                vmem_limit_bytes=VMEM_LIMIT_BYTES)


def _proj_body(x_ref, cs_ref, ng_ref, wcq_ref, wkv_ref, wa_ref, wb_ref, qlg_ref, wuq_ref,
               kvlg_ref, wukv_ref, qg_ref, kg_ref, q_ref, k_ref, v_ref, u_ref):
    x = x_ref[0]
    inv = lax.rsqrt(jnp.mean(x * x, axis=-1, keepdims=True) + EPS)
    h = (x * inv * ng_ref[...]).astype(BF16)
    scale = math.log2(math.e) / math.sqrt(QK_DIM)
    cs = cs_ref[...]

    cq = jnp.dot(h, wcq_ref[...], preferred_element_type=F32)
    cq_inv = lax.rsqrt(jnp.mean(cq * cq, axis=-1, keepdims=True) + EPS)
    cqn = (cq * cq_inv * qlg_ref[...]).astype(BF16)
    qa = jnp.dot(cqn, wuq_ref[...], preferred_element_type=F32)
    q_cs = cs * qg_ref[1:2, :]
    for hh in range(N_HEADS):
        nope = qa[:, LANES * hh:LANES * (hh + 1)]
        rt = qa[:, ATTN_WIDTH + LANES * hh:ATTN_WIDTH + LANES * (hh + 1)]
        ss = jnp.sum(nope * nope + 0.5 * (rt * rt), axis=-1, keepdims=True)
        inv_h = lax.rsqrt(ss * (1.0 / QK_DIM) + EPS) * scale
        q_ref[0, hh, :, 0:LANES] = (nope * inv_h * qg_ref[0:1, :]).astype(BF16)
        t = rt * q_cs
        r = t + pltpu.roll(t, ROPE, 1)
        q_ref[0, hh, :, LANES:2 * LANES] = (r * inv_h).astype(BF16)

    kvin = jnp.dot(h, wkv_ref[...], preferred_element_type=F32)
    ckv = kvin[:, :KV_RANK]
    krt = kvin[:, KV_RANK:KV_RANK + LANES]
    ckv_inv = lax.rsqrt(jnp.mean(ckv * ckv, axis=-1, keepdims=True) + EPS)
    ckvn = (ckv * ckv_inv * kvlg_ref[...]).astype(BF16)
    kva = jnp.dot(ckvn, wukv_ref[...], preferred_element_type=F32)
    kr_sq = 0.5 * (krt * krt)
    tk = krt * (cs * kg_ref[1:2, :])
    rk = tk + pltpu.roll(tk, ROPE, 1)
    lane = lax.broadcasted_iota(jnp.int32, rk.shape, 1)
    rk = jnp.where(lane < ROPE, rk, 0.0)
    for hh in range(N_HEADS):
        kn = kva[:, LANES * hh:LANES * (hh + 1)]
        ss = jnp.sum(kn * kn + kr_sq, axis=-1, keepdims=True)
        inv_h = lax.rsqrt(ss * (1.0 / QK_DIM) + EPS)
        k_ref[0, hh, :, 0:LANES] = (kn * inv_h * kg_ref[0:1, :]).astype(BF16)
        k_ref[0, hh, :, LANES:2 * LANES] = (rk * inv_h).astype(BF16)
        v_ref[0, hh] = kva[:, ATTN_WIDTH + LANES * hh:ATTN_WIDTH + LANES * (hh + 1)].astype(BF16)

    a = jnp.dot(h, wa_ref[...], preferred_element_type=F32)
    b = jnp.dot(h, wb_ref[...], preferred_element_type=F32)
    u_ref[0] = (a * jax.nn.sigmoid(b)).astype(BF16)


def _proj_call(x, cs, w):
    B, S, D = x.shape
    tm = TM_PROJ
    heads = lambda width: pl.BlockSpec((1, N_HEADS, tm, width), lambda b, i: (b, 0, i, 0))
    in_specs = [
        pl.BlockSpec((1, tm, D), lambda b, i: (b, i, 0)),
        pl.BlockSpec((tm, LANES), lambda b, i: (i, 0)),
        _const_spec((1, D)),
        _const_spec((D, Q_RANK)),
        _const_spec((D, KV_RANK + LANES)),
        _const_spec((D, CONV_W)),
        _const_spec((D, CONV_W)),
        _const_spec((1, Q_RANK)),
        _const_spec((Q_RANK, 2 * ATTN_WIDTH)),
        _const_spec((1, KV_RANK)),
        _const_spec((KV_RANK, 2 * ATTN_WIDTH)),
        _const_spec((2, LANES)),
        _const_spec((2, LANES)),
    ]
    out_specs = [heads(QK_PAD), heads(QK_PAD), heads(DV),
                 pl.BlockSpec((1, tm, CONV_W), lambda b, i: (b, i, 0))]
    out_shape = [
        jax.ShapeDtypeStruct((B, N_HEADS, S, QK_PAD), BF16),
        jax.ShapeDtypeStruct((B, N_HEADS, S, QK_PAD), BF16),
        jax.ShapeDtypeStruct((B, N_HEADS, S, DV), BF16),
        jax.ShapeDtypeStruct((B, S, CONV_W), BF16),
    ]
    return pl.pallas_call(
        _proj_body,
        grid=(B, S // tm),
        in_specs=in_specs,
        out_specs=out_specs,
        out_shape=out_shape,
        compiler_params=_params(),
        name="mla_conv_proj",
    )(x, cs, w["norm_g"], w["w_cq"], w["w_kvin"], w["w_a"], w["w_b"], w["q_lora_g"], w["w_uq"],
      w["kv_lora_g"], w["w_ukv"], w["q_head_g"], w["k_head_g"])


def _attn_body(q_ref, k_ref, v_ref, u_ref, dw_ref, dwb_ref, o_ref, c_ref,
               vaug_ref, s_ref, p_ref, ubuf_ref):
    S = k_ref.shape[2]
    tq = s_ref.shape[2]
    vaug_ref[:, 0:DV] = v_ref[0, 0]
    vaug_ref[:, DV:2 * DV] = jnp.ones((S, DV), BF16)

    n_rounds = S // (2 * tq)

    ubuf_ref[0:HALO, :] = jnp.zeros((HALO, LANES), F32)
    ubuf_ref[HALO:HALO + S, :] = u_ref[0].astype(F32)
    ubuf_ref[HALO + S:2 * HALO + S, :] = jnp.zeros((HALO, LANES), F32)

    chunks_per_block = S // CONV_ROWS // (2 * n_rounds)
    anchor_rows = tq // chunks_per_block
    assert chunks_per_block * 2 * n_rounds * CONV_ROWS == S and anchor_rows % 8 == 0

    def conv_chunk(c):
        r0 = c * CONV_ROWS
        acc = jnp.zeros((CONV_ROWS, LANES), F32)
        for kk in range(CONV_K):
            start = r0 + (HALO - CONV_PAD) + kk
            acc = acc + dw_ref[kk:kk + 1, :] * ubuf_ref[start:start + CONV_ROWS, :]
        out = acc + dwb_ref[...]
        c_ref[0, r0:r0 + CONV_ROWS, :] = out
        bits = lax.bitcast_convert_type(out[0:8, :], jnp.uint32)
        return (bits >> 16) >> 16

    def rows(r, j):
        return slice((2 * r + j) * tq, (2 * r + j + 1) * tq)

    def qk(r):
        for j in range(2):
            s_ref[r % 2, j] = lax.dot_general(q_ref[0, 0, rows(r, j), :], k_ref[0, 0],
                                              (((1,), (1,)), ((), ())), preferred_element_type=F32)

    def softmax(r):
        for j in range(2):
            for i in range(chunks_per_block):
                zero = conv_chunk((2 * r + j) * chunks_per_block + i)
                a0 = i * anchor_rows
                head = lax.bitcast_convert_type(s_ref[r % 2, j, a0:a0 + 8, 0:LANES], jnp.uint32)
                s_ref[r % 2, j, a0:a0 + 8, 0:LANES] = lax.bitcast_convert_type(head | zero, F32)
            s = s_ref[r % 2, j]
            p_ref[r % 2, j] = jnp.exp2(s - jnp.max(s, axis=-1, keepdims=True)).astype(BF16)

    def pv(r):
        os_ = [jnp.dot(p_ref[r % 2, j], vaug_ref[...], preferred_element_type=F32) for j in range(2)]
        for j in range(2):
            o = os_[j]
            o_ref[0, rows(r, j), :] = (o[:, 0:DV] / o[:, DV:2 * DV]).astype(BF16)

    qk(0)
    for r in range(n_rounds):
        if r + 1 < n_rounds:
            qk(r + 1)
        softmax(r)
        pv(r)


def _attn_call(q, k, v, u, w):
    B, H, S, _ = q.shape
    head = lambda width: pl.BlockSpec((1, 1, S, width), lambda b, h: (b, h, 0, 0))
    slab = pl.BlockSpec((1, S, LANES), lambda b, h: (b, 0, h))
    return pl.pallas_call(
        _attn_body,
        grid=(B, H),
        in_specs=[
            head(QK_PAD), head(QK_PAD), head(DV), slab,
            pl.BlockSpec((CONV_K, LANES), lambda b, h: (0, h)),
            pl.BlockSpec((1, LANES), lambda b, h: (0, h)),
        ],
        out_specs=[slab, slab],
        out_shape=[jax.ShapeDtypeStruct((B, S, H * DV), BF16),
                   jax.ShapeDtypeStruct((B, S, CONV_W), F32)],
        scratch_shapes=[pltpu.VMEM((S, 2 * DV), BF16),
                        pltpu.VMEM((2, 2, TQ, S), F32),
                        pltpu.VMEM((2, 2, TQ, S), BF16),
                        pltpu.VMEM((S + 2 * HALO, LANES), F32)],
        compiler_params=_params(),
        name="mla_attention_conv",
    )(q, k, v, u, w["dw_kernel"], w["dw_bias"])


def _merge_body(x_ref, attn_ref, c_ref, ng_ref, wg_ref, wo_ref, wpw_ref, wout_ref,
                lng_ref, lnb_ref, o_ref):
    x = x_ref[0]
    inv = lax.rsqrt(jnp.mean(x * x, axis=-1, keepdims=True) + EPS)
    h = (x * inv * ng_ref[...]).astype(BF16)

    ga = jax.nn.silu(jnp.dot(h, wg_ref[:, 0:D_MODEL], preferred_element_type=F32))
    ya_in = (attn_ref[0].astype(F32) * ga).astype(BF16)
    y_a = jnp.dot(ya_in, wo_ref[...], preferred_element_type=F32)

    c = c_ref[0]
    mu = jnp.mean(c, axis=-1, keepdims=True)
    cc = c - mu
    var = jnp.mean(cc * cc, axis=-1, keepdims=True)
    ln = cc * lax.rsqrt(var + EPS) * lng_ref[...] + lnb_ref[...]
    gc = jax.nn.silu(jnp.dot(h, wg_ref[:, D_MODEL:2 * D_MODEL], preferred_element_type=F32))
    yc_in = (jax.nn.silu(ln) * gc).astype(BF16)
    y_c = jnp.dot(yc_in, wpw_ref[...], preferred_element_type=F32)

    ma = jax.nn.sigmoid(jnp.dot(h, wg_ref[:, 2 * D_MODEL:3 * D_MODEL], preferred_element_type=F32))
    mc = jax.nn.sigmoid(jnp.dot(h, wg_ref[:, 3 * D_MODEL:4 * D_MODEL], preferred_element_type=F32))
    merged = (ma * y_a + mc * y_c).astype(BF16)
    o_ref[0] = x + jnp.dot(merged, wout_ref[...], preferred_element_type=F32)


def _merge_call(x, attn, c, w):
    B, S, D = x.shape
    tm = TM_MERGE
    tile = lambda width: pl.BlockSpec((1, tm, width), lambda b, i: (b, i, 0))
    in_specs = [
        tile(D), tile(ATTN_WIDTH), tile(CONV_W),
        _const_spec((1, D)),
        _const_spec((D, 4 * D_MODEL)),
        _const_spec((ATTN_WIDTH, D_MODEL)),
        _const_spec((CONV_W, D_MODEL)),
        _const_spec((D_MODEL, D_MODEL)),
        _const_spec((1, CONV_W)),
        _const_spec((1, CONV_W)),
    ]
    return pl.pallas_call(
        _merge_body,
        grid=(B, S // tm),
        in_specs=in_specs,
        out_specs=tile(D),
        out_shape=jax.ShapeDtypeStruct((B, S, D), F32),
        compiler_params=_params(),
        name="gate_merge",
    )(x, attn, c, w["norm_g"], w["w_gates"], w["w_o_attn"], w["w_pw2"], w["w_out"],
      w["conv_ln_g"], w["conv_ln_b"])


def _half_swap(t, sign):
    half = t.shape[-1] // 2
    return jnp.concatenate([sign * t[..., half:], t[..., :half]], axis=-1)


def _prep_weights(norm_g, w_in, q_lora_g, w_uq, kv_lora_g, w_ukv, q_head_g, k_head_g, w_o_attn,
                  dw_kernel, dw_bias, conv_ln_g, conv_ln_b, w_pw2, w_out):
    offs = [0]
    for nsz in SPLITS:
        offs.append(offs[-1] + nsz)
    col = lambda g: w_in[:, offs[g]:offs[g + 1]]
    w_kr = col(2)
    w_conv = col(4)
    uq = w_uq.reshape(Q_RANK, N_HEADS, QK_DIM)
    uq_rope = uq[:, :, NOPE:]
    uq_tiles = jnp.concatenate([uq_rope, _half_swap(uq_rope, -1.0)], axis=-1)
    ukv = w_ukv.reshape(KV_RANK, N_HEADS, NOPE + DV)

    def head_gain(g):
        g_rope = g[NOPE:]
        return jnp.stack([g[:NOPE], jnp.concatenate([g_rope, _half_swap(g_rope, 1.0)])])

    return {
        "norm_g": norm_g.reshape(1, D_MODEL),
        "w_cq": col(0).astype(BF16),
        "w_kvin": jnp.concatenate([col(1), w_kr, _half_swap(w_kr, -1.0)], axis=1).astype(BF16),
        "w_a": w_conv[:, :CONV_W].astype(BF16),
        "w_b": w_conv[:, CONV_W:].astype(BF16),
        "q_lora_g": q_lora_g.reshape(1, Q_RANK),
        "w_uq": jnp.concatenate([uq[:, :, :NOPE].reshape(Q_RANK, ATTN_WIDTH),
                                 uq_tiles.reshape(Q_RANK, ATTN_WIDTH)], axis=1).astype(BF16),
        "kv_lora_g": kv_lora_g.reshape(1, KV_RANK),
        "w_ukv": jnp.concatenate([ukv[:, :, :NOPE].reshape(KV_RANK, ATTN_WIDTH),
                                  ukv[:, :, NOPE:].reshape(KV_RANK, ATTN_WIDTH)], axis=1).astype(BF16),
        "q_head_g": head_gain(q_head_g),
        "k_head_g": head_gain(k_head_g),
        "w_gates": jnp.concatenate([col(3), col(5), col(6), col(7)], axis=1).astype(BF16),
        "w_o_attn": w_o_attn.astype(BF16),
        "w_pw2": w_pw2.astype(BF16),
        "w_out": w_out.astype(BF16),
        "dw_kernel": dw_kernel,
        "dw_bias": dw_bias.reshape(1, CONV_W),
        "conv_ln_g": conv_ln_g.reshape(1, CONV_W),
        "conv_ln_b": conv_ln_b.reshape(1, CONV_W),
    }


def _rope_table(seq_len):
    half = ROPE // 2
    inv_freq = 1.0 / (ROPE_THETA ** (jnp.arange(half, dtype=F32) / half))
    ang = jnp.arange(seq_len, dtype=F32)[:, None] * inv_freq[None, :]
    cos, sin = jnp.cos(ang), jnp.sin(ang)
    return jnp.concatenate([cos, cos, sin, sin], axis=1)


def _layer(x, w):
    cs = _rope_table(x.shape[1])
    q, k, v, u = _proj_call(x, cs, w)
    attn, c = _attn_call(q, k, v, u, w)
    return _merge_call(x, attn, c, w)


def kernel(x_prompt, x_sample, norm_g, w_in, q_lora_g, w_uq, kv_lora_g, w_ukv, q_head_g, k_head_g,
           w_o_attn, dw_kernel, dw_bias, conv_ln_g, conv_ln_b, w_pw2, w_out):
    depth = norm_g.shape[0]
    y_prompt, y_sample = x_prompt, x_sample
    for l in range(depth):
        w = _prep_weights(norm_g[l], w_in[l], q_lora_g[l], w_uq[l], kv_lora_g[l], w_ukv[l],
                          q_head_g[l], k_head_g[l], w_o_attn[l], dw_kernel[l], dw_bias[l],
                          conv_ln_g[l], conv_ln_b[l], w_pw2[l], w_out[l])
        y_prompt = _layer(y_prompt, w)
        y_sample = _layer(y_sample, w)
    return (y_prompt, y_sample)
```

```python
import math

import jax
import jax.numpy as jnp
from jax import lax
from jax.experimental import pallas as pl
from jax.experimental.pallas import tpu as pltpu

D_MODEL = 1024
N_HEADS = 8
NOPE = 128
ROPE = 64
QK_DIM = NOPE + ROPE
QK_PAD = 256
DV = 128
Q_RANK = 384
KV_RANK = 256
ATTN_WIDTH = N_HEADS * DV
CONV_W = D_MODEL
CONV_K = 31
CONV_PAD = (CONV_K - 1) // 2
ROPE_THETA = 10000.0
EPS = 1e-6
LANES = 128
HALO = 16
SPLITS = (Q_RANK, KV_RANK, ROPE, ATTN_WIDTH, 2 * CONV_W, CONV_W, D_MODEL, D_MODEL)

VMEM_LIMIT_BYTES = 56 * 1024 * 1024
TM_PROJ = 512
TM_MERGE = 512
TQ = 256
CONV_ROWS = 32
CONV_GROUP = 8
UBUF_EXTRA = 8

F32 = jnp.float32
BF16 = jnp.bfloat16
U32 = jnp.uint32

assert N_HEADS * LANES == CONV_W


def _const_spec(shape):
    return pl.BlockSpec(shape, lambda *_: (0,) * len(shape), pipeline_mode=pl.Buffered(1))


def _params():
    return pltpu.CompilerParams(dimension_semantics=("parallel", "parallel"),
                                vmem_limit_bytes=VMEM_LIMIT_BYTES)


def _proj_body(x_ref, cs_ref, ng_ref, wcq_ref, wkv_ref, wa_ref, wb_ref, qlg_ref, wuq_ref,
               kvlg_ref, wukv_ref, qg_ref, kg_ref, q_ref, k_ref, v_ref, u_ref):
    x = x_ref[0]
    inv = lax.rsqrt(jnp.mean(x * x, axis=-1, keepdims=True) + EPS)
    h = (x * inv * ng_ref[...]).astype(BF16)
    scale = math.log2(math.e) / math.sqrt(QK_DIM)
    cs = cs_ref[...]

    cq = jnp.dot(h, wcq_ref[...], preferred_element_type=F32)
    cq_inv = lax.rsqrt(jnp.mean(cq * cq, axis=-1, keepdims=True) + EPS)
    cqn = (cq * cq_inv * qlg_ref[...]).astype(BF16)
    qa = jnp.dot(cqn, wuq_ref[...], preferred_element_type=F32)
    q_cs = cs * qg_ref[1:2, :]
    for hh in range(N_HEADS):
        nope = qa[:, LANES * hh:LANES * (hh + 1)]
        rt = qa[:, ATTN_WIDTH + LANES * hh:ATTN_WIDTH + LANES * (hh + 1)]
        ss = jnp.sum(nope * nope + 0.5 * (rt * rt), axis=-1, keepdims=True)
        inv_h = lax.rsqrt(ss * (1.0 / QK_DIM) + EPS) * scale
        q_ref[0, hh, :, 0:LANES] = (nope * inv_h * qg_ref[0:1, :]).astype(BF16)
        t = rt * q_cs
        r = t + pltpu.roll(t, ROPE, 1)
        q_ref[0, hh, :, LANES:2 * LANES] = (r * inv_h).astype(BF16)

    kvin = jnp.dot(h, wkv_ref[...], preferred_element_type=F32)
    ckv = kvin[:, :KV_RANK]
    krt = kvin[:, KV_RANK:KV_RANK + LANES]
    ckv_inv = lax.rsqrt(jnp.mean(ckv * ckv, axis=-1, keepdims=True) + EPS)
    ckvn = (ckv * ckv_inv * kvlg_ref[...]).astype(BF16)
    kva = jnp.dot(ckvn, wukv_ref[...], preferred_element_type=F32)
    kr_sq = 0.5 * (krt * krt)
    tk = krt * (cs * kg_ref[1:2, :])
    rk = tk + pltpu.roll(tk, ROPE, 1)
    lane = lax.broadcasted_iota(jnp.int32, rk.shape, 1)
    rk = jnp.where(lane < ROPE, rk, 0.0)
    for hh in range(N_HEADS):
        kn = kva[:, LANES * hh:LANES * (hh + 1)]
        ss = jnp.sum(kn * kn + kr_sq, axis=-1, keepdims=True)
        inv_h = lax.rsqrt(ss * (1.0 / QK_DIM) + EPS)
        k_ref[0, hh, :, 0:LANES] = (kn * inv_h * kg_ref[0:1, :]).astype(BF16)
        k_ref[0, hh, :, LANES:2 * LANES] = (rk * inv_h).astype(BF16)
        v_ref[0, hh] = kva[:, ATTN_WIDTH + LANES * hh:ATTN_WIDTH + LANES * (hh + 1)].astype(BF16)

    a = jnp.dot(h, wa_ref[...], preferred_element_type=F32)
    b = jnp.dot(h, wb_ref[...], preferred_element_type=F32)
    u_ref[0] = (a * jax.nn.sigmoid(b)).astype(BF16)


def _proj_call(x, cs, w):
    B, S, D = x.shape
    tm = TM_PROJ
    heads = lambda width: pl.BlockSpec((1, N_HEADS, tm, width), lambda b, i: (b, 0, i, 0))
    in_specs = [
        pl.BlockSpec((1, tm, D), lambda b, i: (b, i, 0)),
        pl.BlockSpec((tm, LANES), lambda b, i: (i, 0)),
        _const_spec((1, D)),
        _const_spec((D, Q_RANK)),
        _const_spec((D, KV_RANK + LANES)),
        _const_spec((D, CONV_W)),
        _const_spec((D, CONV_W)),
        _const_spec((1, Q_RANK)),
        _const_spec((Q_RANK, 2 * ATTN_WIDTH)),
        _const_spec((1, KV_RANK)),
        _const_spec((KV_RANK, 2 * ATTN_WIDTH)),
        _const_spec((2, LANES)),
        _const_spec((2, LANES)),
    ]
    out_specs = [heads(QK_PAD), heads(QK_PAD), heads(DV),
                 pl.BlockSpec((1, tm, CONV_W), lambda b, i: (b, i, 0))]
    out_shape = [
        jax.ShapeDtypeStruct((B, N_HEADS, S, QK_PAD), BF16),
        jax.ShapeDtypeStruct((B, N_HEADS, S, QK_PAD), BF16),
        jax.ShapeDtypeStruct((B, N_HEADS, S, DV), BF16),
        jax.ShapeDtypeStruct((B, S, CONV_W), BF16),
    ]
    return pl.pallas_call(
        _proj_body,
        grid=(B, S // tm),
        in_specs=in_specs,
        out_specs=out_specs,
        out_shape=out_shape,
        compiler_params=_params(),
        name="mla_conv_proj",
    )(x, cs, w["norm_g"], w["w_cq"], w["w_kvin"], w["w_a"], w["w_b"], w["q_lora_g"], w["w_uq"],
      w["kv_lora_g"], w["w_ukv"], w["q_head_g"], w["k_head_g"])


def _attn_body(q_ref, k_ref, v_ref, u_ref, dwp_ref, dwb_ref, o_ref, c_ref,
               vaug_ref, s_ref, p_ref, ubuf_ref, pair_ref):
    S = k_ref.shape[2]
    tq = s_ref.shape[2]
    vaug_ref[:, 0:DV] = v_ref[0, 0]
    vaug_ref[:, DV:2 * DV] = jnp.ones((S, DV), BF16)

    n_rounds = S // (2 * tq)

    ubuf_ref[0:HALO, :] = jnp.zeros((HALO, LANES), F32)
    ubuf_ref[HALO:HALO + S, :] = u_ref[0].astype(F32)
    ubuf_ref[HALO + S:2 * HALO + S + UBUF_EXTRA, :] = jnp.zeros((HALO + UBUF_EXTRA, LANES), F32)
    n_words = S // 2 + HALO
    bits = lambda ref_slice: lax.bitcast_convert_type(ref_slice, U32)
    even = bits(ubuf_ref[pl.ds(0, n_words, stride=2), :])
    odd = bits(ubuf_ref[pl.ds(1, n_words, stride=2), :])
    even2 = bits(ubuf_ref[pl.ds(2, n_words, stride=2), :])
    pair_ref[0] = (even >> 16) | odd
    pair_ref[1] = (odd >> 16) | even2

    chunks_per_block = S // CONV_ROWS // (2 * n_rounds)
    anchor_rows = tq // chunks_per_block
    assert chunks_per_block * 2 * n_rounds * CONV_ROWS == S and anchor_rows % 8 == 0

    def conv_chunk(c):
        r0 = c * CONV_ROWS

        def product(kk):
            start = r0 + (HALO - CONV_PAD) + kk
            win = pltpu.bitcast(pair_ref[start % 2, pl.ds(start // 2, CONV_ROWS // 2), :], BF16)
            tap = jnp.broadcast_to(dwp_ref[kk:kk + 1, :], (CONV_ROWS // 2, LANES))
            return win * pltpu.bitcast(tap, BF16)

        def tree(terms):
            while len(terms) > 1:
                terms = [a + b for a, b in zip(terms[0::2], terms[1::2])] + terms[len(terms) & ~1:]
            return terms[0]

        out = dwb_ref[...]
        for g in range(0, CONV_K, CONV_GROUP):
            out = out + tree([product(kk) for kk in range(g, min(g + CONV_GROUP, CONV_K))]).astype(F32)
        c_ref[0, r0:r0 + CONV_ROWS, :] = out
        bits = lax.bitcast_convert_type(out[0:8, :], jnp.uint32)
        return (bits >> 16) >> 16

    def rows(r, j):
        return slice((2 * r + j) * tq, (2 * r + j + 1) * tq)

    def qk(r):
        for j in range(2):
            s_ref[r % 2, j] = lax.dot_general(q_ref[0, 0, rows(r, j), :], k_ref[0, 0],
                                              (((1,), (1,)), ((), ())), preferred_element_type=F32)

    def softmax(r):
        for j in range(2):
            for i in range(chunks_per_block):
                zero = conv_chunk((2 * r + j) * chunks_per_block + i)
                a0 = i * anchor_rows
                head = lax.bitcast_convert_type(s_ref[r % 2, j, a0:a0 + 8, 0:LANES], jnp.uint32)
                s_ref[r % 2, j, a0:a0 + 8, 0:LANES] = lax.bitcast_convert_type(head | zero, F32)
            s = s_ref[r % 2, j]
            p_ref[r % 2, j] = jnp.exp2(s - jnp.max(s, axis=-1, keepdims=True)).astype(BF16)

    def pv(r):
        os_ = [jnp.dot(p_ref[r % 2, j], vaug_ref[...], preferred_element_type=F32) for j in range(2)]
        for j in range(2):
            o = os_[j]
            o_ref[0, rows(r, j), :] = (o[:, 0:DV] / o[:, DV:2 * DV]).astype(BF16)

    qk(0)
    for r in range(n_rounds):
        if r + 1 < n_rounds:
            qk(r + 1)
        softmax(r)
        pv(r)


def _attn_call(q, k, v, u, w):
    B, H, S, _ = q.shape
    head = lambda width: pl.BlockSpec((1, 1, S, width), lambda b, h: (b, h, 0, 0))
    slab = pl.BlockSpec((1, S, LANES), lambda b, h: (b, 0, h))
    return pl.pallas_call(
        _attn_body,
        grid=(B, H),
        in_specs=[
            head(QK_PAD), head(QK_PAD), head(DV), slab,
            pl.BlockSpec((CONV_K, LANES), lambda b, h: (0, h)),
            pl.BlockSpec((1, LANES), lambda b, h: (0, h)),
        ],
        out_specs=[slab, slab],
        out_shape=[jax.ShapeDtypeStruct((B, S, H * DV), BF16),
                   jax.ShapeDtypeStruct((B, S, CONV_W), F32)],
        scratch_shapes=[pltpu.VMEM((S, 2 * DV), BF16),
                        pltpu.VMEM((2, 2, TQ, S), F32),
                        pltpu.VMEM((2, 2, TQ, S), BF16),
                        pltpu.VMEM((S + 2 * HALO + UBUF_EXTRA, LANES), F32),
                        pltpu.VMEM((2, S // 2 + HALO, LANES), U32)],
        compiler_params=_params(),
        name="mla_attention_conv",
    )(q, k, v, u, w["dw_packed"], w["dw_bias"])


def _merge_body(x_ref, attn_ref, c_ref, ng_ref, wg_ref, wo_ref, wpw_ref, wout_ref,
                lng_ref, lnb_ref, o_ref):
    x = x_ref[0]
    inv = lax.rsqrt(jnp.mean(x * x, axis=-1, keepdims=True) + EPS)
    h = (x * inv * ng_ref[...]).astype(BF16)

    ga = jax.nn.silu(jnp.dot(h, wg_ref[:, 0:D_MODEL], preferred_element_type=F32))
    ya_in = (attn_ref[0].astype(F32) * ga).astype(BF16)
    y_a = jnp.dot(ya_in, wo_ref[...], preferred_element_type=F32)

    c = c_ref[0]
    mu = jnp.mean(c, axis=-1, keepdims=True)
    cc = c - mu
    var = jnp.mean(cc * cc, axis=-1, keepdims=True)
    ln = cc * lax.rsqrt(var + EPS) * lng_ref[...] + lnb_ref[...]
    gc = jax.nn.silu(jnp.dot(h, wg_ref[:, D_MODEL:2 * D_MODEL], preferred_element_type=F32))
    yc_in = (jax.nn.silu(ln) * gc).astype(BF16)
    y_c = jnp.dot(yc_in, wpw_ref[...], preferred_element_type=F32)

    ma = jax.nn.sigmoid(jnp.dot(h, wg_ref[:, 2 * D_MODEL:3 * D_MODEL], preferred_element_type=F32))
    mc = jax.nn.sigmoid(jnp.dot(h, wg_ref[:, 3 * D_MODEL:4 * D_MODEL], preferred_element_type=F32))
    merged = (ma * y_a + mc * y_c).astype(BF16)
    o_ref[0] = x + jnp.dot(merged, wout_ref[...], preferred_element_type=F32)


def _merge_call(x, attn, c, w):
    B, S, D = x.shape
    tm = TM_MERGE
    tile = lambda width: pl.BlockSpec((1, tm, width), lambda b, i: (b, i, 0))
    in_specs = [
        tile(D), tile(ATTN_WIDTH), tile(CONV_W),
        _const_spec((1, D)),
        _const_spec((D, 4 * D_MODEL)),
        _const_spec((ATTN_WIDTH, D_MODEL)),
        _const_spec((CONV_W, D_MODEL)),
        _const_spec((D_MODEL, D_MODEL)),
        _const_spec((1, CONV_W)),
        _const_spec((1, CONV_W)),
    ]
    return pl.pallas_call(
        _merge_body,
        grid=(B, S // tm),
        in_specs=in_specs,
        out_specs=tile(D),
        out_shape=jax.ShapeDtypeStruct((B, S, D), F32),
        compiler_params=_params(),
        name="gate_merge",
    )(x, attn, c, w["norm_g"], w["w_gates"], w["w_o_attn"], w["w_pw2"], w["w_out"],
      w["conv_ln_g"], w["conv_ln_b"])


def _half_swap(t, sign):
    half = t.shape[-1] // 2
    return jnp.concatenate([sign * t[..., half:], t[..., :half]], axis=-1)


def _pack_bf16_pairs(t):
    half = lax.bitcast_convert_type(t.astype(BF16), jnp.uint16).astype(U32)
    return half | (half << 16)


def _prep_weights(norm_g, w_in, q_lora_g, w_uq, kv_lora_g, w_ukv, q_head_g, k_head_g, w_o_attn,
                  dw_kernel, dw_bias, conv_ln_g, conv_ln_b, w_pw2, w_out):
    offs = [0]
    for nsz in SPLITS:
        offs.append(offs[-1] + nsz)
    col = lambda g: w_in[:, offs[g]:offs[g + 1]]
    w_kr = col(2)
    w_conv = col(4)
    uq = w_uq.reshape(Q_RANK, N_HEADS, QK_DIM)
    uq_rope = uq[:, :, NOPE:]
    uq_tiles = jnp.concatenate([uq_rope, _half_swap(uq_rope, -1.0)], axis=-1)
    ukv = w_ukv.reshape(KV_RANK, N_HEADS, NOPE + DV)

    def head_gain(g):
        g_rope = g[NOPE:]
        return jnp.stack([g[:NOPE], jnp.concatenate([g_rope, _half_swap(g_rope, 1.0)])])

    return {
        "norm_g": norm_g.reshape(1, D_MODEL),
        "w_cq": col(0).astype(BF16),
        "w_kvin": jnp.concatenate([col(1), w_kr, _half_swap(w_kr, -1.0)], axis=1).astype(BF16),
        "w_a": w_conv[:, :CONV_W].astype(BF16),
        "w_b": w_conv[:, CONV_W:].astype(BF16),
        "q_lora_g": q_lora_g.reshape(1, Q_RANK),
        "w_uq": jnp.concatenate([uq[:, :, :NOPE].reshape(Q_RANK, ATTN_WIDTH),
                                 uq_tiles.reshape(Q_RANK, ATTN_WIDTH)], axis=1).astype(BF16),
        "kv_lora_g": kv_lora_g.reshape(1, KV_RANK),
        "w_ukv": jnp.concatenate([ukv[:, :, :NOPE].reshape(KV_RANK, ATTN_WIDTH),
                                  ukv[:, :, NOPE:].reshape(KV_RANK, ATTN_WIDTH)], axis=1).astype(BF16),
        "q_head_g": head_gain(q_head_g),
        "k_head_g": head_gain(k_head_g),
        "w_gates": jnp.concatenate([col(3), col(5), col(6), col(7)], axis=1).astype(BF16),
        "w_o_attn": w_o_attn.astype(BF16),
        "w_pw2": w_pw2.astype(BF16),
        "w_out": w_out.astype(BF16),
        "dw_packed": _pack_bf16_pairs(dw_kernel),
        "dw_bias": dw_bias.reshape(1, CONV_W),
        "conv_ln_g": conv_ln_g.reshape(1, CONV_W),
        "conv_ln_b": conv_ln_b.reshape(1, CONV_W),
    }


def _rope_table(seq_len):
    half = ROPE // 2
    inv_freq = 1.0 / (ROPE_THETA ** (jnp.arange(half, dtype=F32) / half))
    ang = jnp.arange(seq_len, dtype=F32)[:, None] * inv_freq[None, :]
    cos, sin = jnp.cos(ang), jnp.sin(ang)
    return jnp.concatenate([cos, cos, sin, sin], axis=1)


def _layer(x, w):
    cs = _rope_table(x.shape[1])
    q, k, v, u = _proj_call(x, cs, w)
    attn, c = _attn_call(q, k, v, u, w)
    return _merge_call(x, attn, c, w)


def kernel(x_prompt, x_sample, norm_g, w_in, q_lora_g, w_uq, kv_lora_g, w_ukv, q_head_g, k_head_g,
           w_o_attn, dw_kernel, dw_bias, conv_ln_g, conv_ln_b, w_pw2, w_out):
    depth = norm_g.shape[0]
    y_prompt, y_sample = x_prompt, x_sample
    for l in range(depth):
        w = _prep_weights(norm_g[l], w_in[l], q_lora_g[l], w_uq[l], kv_lora_g[l], w_ukv[l],
                          q_head_g[l], k_head_g[l], w_o_attn[l], dw_kernel[l], dw_bias[l],
                          conv_ln_g[l], conv_ln_b[l], w_pw2[l], w_out[l])
        y_prompt = _layer(y_prompt, w)
        y_sample = _layer(y_sample, w)
    return (y_prompt, y_sample)
```

```python
import math

import jax
import jax.numpy as jnp
from jax import lax
from jax.experimental import pallas as pl
from jax.experimental.pallas import tpu as pltpu

D_MODEL = 1024
N_HEADS = 8
NOPE = 128
ROPE = 64
QK_DIM = NOPE + ROPE
QK_PAD = 256
DV = 128
Q_RANK = 384
KV_RANK = 256
ATTN_WIDTH = N_HEADS * DV
CONV_W = D_MODEL
CONV_K = 31
CONV_PAD = (CONV_K - 1) // 2
ROPE_THETA = 10000.0
EPS = 1e-6
LANES = 128
HALO = 16
SPLITS = (Q_RANK, KV_RANK, ROPE, ATTN_WIDTH, 2 * CONV_W, CONV_W, D_MODEL, D_MODEL)

VMEM_LIMIT_BYTES = 56 * 1024 * 1024
TM_PROJ = 512
TM_MERGE = 512
TQ = 256
CONV_ROWS = 32
CONV_GROUP = 8
UBUF_EXTRA = 8

F32 = jnp.float32
BF16 = jnp.bfloat16
U32 = jnp.uint32

assert N_HEADS * LANES == CONV_W


def _const_spec(shape):
    return pl.BlockSpec(shape, lambda *_: (0,) * len(shape), pipeline_mode=pl.Buffered(1))


def _params():
    return pltpu.CompilerParams(dimension_semantics=("parallel", "parallel"),
                                vmem_limit_bytes=VMEM_LIMIT_BYTES)


def _proj_body(x_ref, cs_ref, ng_ref, wcq_ref, wkv_ref, wa_ref, wb_ref, qlg_ref, wuq_ref,
               kvlg_ref, wukv_ref, qg_ref, kg_ref, q_ref, k_ref, v_ref, u_ref):
    x = x_ref[0]
    inv = lax.rsqrt(jnp.mean(x * x, axis=-1, keepdims=True) + EPS)
    h = (x * inv * ng_ref[...]).astype(BF16)
    scale = math.log2(math.e) / math.sqrt(QK_DIM)
    cs = cs_ref[...]

    cq = jnp.dot(h, wcq_ref[...], preferred_element_type=F32)
    cq_inv = lax.rsqrt(jnp.mean(cq * cq, axis=-1, keepdims=True) + EPS)
    cqn = (cq * cq_inv * qlg_ref[...]).astype(BF16)
    qa = jnp.dot(cqn, wuq_ref[...], preferred_element_type=F32)
    q_cs = cs * qg_ref[1:2, :]
    for hh in range(N_HEADS):
        nope = qa[:, LANES * hh:LANES * (hh + 1)]
        rt = qa[:, ATTN_WIDTH + LANES * hh:ATTN_WIDTH + LANES * (hh + 1)]
        ss = jnp.sum(nope * nope + 0.5 * (rt * rt), axis=-1, keepdims=True)
        inv_h = lax.rsqrt(ss * (1.0 / QK_DIM) + EPS) * scale
        q_ref[0, hh, :, 0:LANES] = (nope * inv_h * qg_ref[0:1, :]).astype(BF16)
        t = rt * q_cs
        r = t + pltpu.roll(t, ROPE, 1)
        q_ref[0, hh, :, LANES:2 * LANES] = (r * inv_h).astype(BF16)

    kvin = jnp.dot(h, wkv_ref[...], preferred_element_type=F32)
    ckv = kvin[:, :KV_RANK]
    krt = kvin[:, KV_RANK:KV_RANK + LANES]
    ckv_inv = lax.rsqrt(jnp.mean(ckv * ckv, axis=-1, keepdims=True) + EPS)
    ckvn = (ckv * ckv_inv * kvlg_ref[...]).astype(BF16)
    kva = jnp.dot(ckvn, wukv_ref[...], preferred_element_type=F32)
    kr_sq = 0.5 * (krt * krt)
    tk = krt * (cs * kg_ref[1:2, :])
    rk = tk + pltpu.roll(tk, ROPE, 1)
    lane = lax.broadcasted_iota(jnp.int32, rk.shape, 1)
    rk = jnp.where(lane < ROPE, rk, 0.0)
    for hh in range(N_HEADS):
        kn = kva[:, LANES * hh:LANES * (hh + 1)]
        ss = jnp.sum(kn * kn + kr_sq, axis=-1, keepdims=True)
        inv_h = lax.rsqrt(ss * (1.0 / QK_DIM) + EPS)
        k_ref[0, hh, :, 0:LANES] = (kn * inv_h * kg_ref[0:1, :]).astype(BF16)
        k_ref[0, hh, :, LANES:2 * LANES] = (rk * inv_h).astype(BF16)
        v_ref[0, hh] = kva[:, ATTN_WIDTH + LANES * hh:ATTN_WIDTH + LANES * (hh + 1)].astype(BF16)

    a = jnp.dot(h, wa_ref[...], preferred_element_type=F32)
    b = jnp.dot(h, wb_ref[...], preferred_element_type=F32)
    u = (a * jax.nn.sigmoid(b)).astype(BF16)
    for hh in range(N_HEADS):
        u_ref[0, hh] = u[:, LANES * hh:LANES * (hh + 1)]


def _proj_call(x, cs, w):
    B, S, D = x.shape
    tm = TM_PROJ
    heads = lambda width: pl.BlockSpec((1, N_HEADS, tm, width), lambda b, i: (b, 0, i, 0))
    in_specs = [
        pl.BlockSpec((1, tm, D), lambda b, i: (b, i, 0)),
        pl.BlockSpec((tm, LANES), lambda b, i: (i, 0)),
        _const_spec((1, D)),
        _const_spec((D, Q_RANK)),
        _const_spec((D, KV_RANK + LANES)),
        _const_spec((D, CONV_W)),
        _const_spec((D, CONV_W)),
        _const_spec((1, Q_RANK)),
        _const_spec((Q_RANK, 2 * ATTN_WIDTH)),
        _const_spec((1, KV_RANK)),
        _const_spec((KV_RANK, 2 * ATTN_WIDTH)),
        _const_spec((2, LANES)),
        _const_spec((2, LANES)),
    ]
    out_specs = [heads(QK_PAD), heads(QK_PAD), heads(DV), heads(LANES)]
    out_shape = [
        jax.ShapeDtypeStruct((B, N_HEADS, S, QK_PAD), BF16),
        jax.ShapeDtypeStruct((B, N_HEADS, S, QK_PAD), BF16),
        jax.ShapeDtypeStruct((B, N_HEADS, S, DV), BF16),
        jax.ShapeDtypeStruct((B, N_HEADS, S, LANES), BF16),
    ]
    return pl.pallas_call(
        _proj_body,
        grid=(B, S // tm),
        in_specs=in_specs,
        out_specs=out_specs,
        out_shape=out_shape,
        compiler_params=_params(),
        name="mla_conv_proj",
    )(x, cs, w["norm_g"], w["w_cq"], w["w_kvin"], w["w_a"], w["w_b"], w["q_lora_g"], w["w_uq"],
      w["kv_lora_g"], w["w_ukv"], w["q_head_g"], w["k_head_g"])


def _attn_body(q_ref, k_ref, v_ref, u_ref, dwp_ref, dwb_ref, o_ref, c_ref,
               vaug_ref, s_ref, p_ref, m_ref, ubuf_ref, pair_ref):
    S = k_ref.shape[2]
    tq = s_ref.shape[2]
    vaug_ref[:, 0:DV] = v_ref[0, 0]
    vaug_ref[:, DV:2 * DV] = jnp.ones((S, DV), BF16)

    n_rounds = S // (2 * tq)

    ubuf_ref[0:HALO, :] = jnp.zeros((HALO, LANES), F32)
    ubuf_ref[HALO:HALO + S, :] = u_ref[0, 0].astype(F32)
    ubuf_ref[HALO + S:2 * HALO + S + UBUF_EXTRA, :] = jnp.zeros((HALO + UBUF_EXTRA, LANES), F32)
    n_words = S // 2 + HALO
    bits = lambda ref_slice: lax.bitcast_convert_type(ref_slice, U32)
    even = bits(ubuf_ref[pl.ds(0, n_words, stride=2), :])
    odd = bits(ubuf_ref[pl.ds(1, n_words, stride=2), :])
    even2 = bits(ubuf_ref[pl.ds(2, n_words, stride=2), :])
    pair_ref[0] = (even >> 16) | odd
    pair_ref[1] = (odd >> 16) | even2

    chunks_per_block = S // CONV_ROWS // (2 * n_rounds)
    anchor_rows = tq // chunks_per_block
    assert chunks_per_block * 2 * n_rounds * CONV_ROWS == S and anchor_rows % 8 == 0

    def conv_chunk(c):
        r0 = c * CONV_ROWS

        def product(kk):
            start = r0 + (HALO - CONV_PAD) + kk
            win = pltpu.bitcast(pair_ref[start % 2, pl.ds(start // 2, CONV_ROWS // 2), :], BF16)
            tap = jnp.broadcast_to(dwp_ref[kk:kk + 1, :], (CONV_ROWS // 2, LANES))
            return win * pltpu.bitcast(tap, BF16)

        def tree(terms):
            while len(terms) > 1:
                terms = [a + b for a, b in zip(terms[0::2], terms[1::2])] + terms[len(terms) & ~1:]
            return terms[0]

        out = dwb_ref[...]
        for g in range(0, CONV_K, CONV_GROUP):
            out = out + tree([product(kk) for kk in range(g, min(g + CONV_GROUP, CONV_K))]).astype(F32)
        c_ref[0, 0, r0:r0 + CONV_ROWS, :] = out
        bits = lax.bitcast_convert_type(out[0:8, :], jnp.uint32)
        return (bits >> 16) >> 16

    def rows(r, j):
        return slice((2 * r + j) * tq, (2 * r + j + 1) * tq)

    def qk(r):
        for j in range(2):
            s = lax.dot_general(q_ref[0, 0, rows(r, j), :], k_ref[0, 0],
                                (((1,), (1,)), ((), ())), preferred_element_type=F32)
            s_ref[r % 2, j] = s
            m_ref[r % 2, j] = jnp.max(s, axis=-1, keepdims=True)

    def softmax(r):
        for j in range(2):
            for i in range(chunks_per_block):
                zero = conv_chunk((2 * r + j) * chunks_per_block + i)
                a0 = i * anchor_rows
                head = lax.bitcast_convert_type(s_ref[r % 2, j, a0:a0 + 8, 0:LANES], jnp.uint32)
                s_ref[r % 2, j, a0:a0 + 8, 0:LANES] = lax.bitcast_convert_type(head | zero, F32)
            s = s_ref[r % 2, j]
            p_ref[r % 2, j] = jnp.exp2(s - m_ref[r % 2, j]).astype(BF16)

    def pv(r):
        os_ = [jnp.dot(p_ref[r % 2, j], vaug_ref[...], preferred_element_type=F32) for j in range(2)]
        for j in range(2):
            o = os_[j]
            o_ref[0, 0, rows(r, j), :] = (o[:, 0:DV] / o[:, DV:2 * DV]).astype(BF16)

    qk(0)
    for r in range(n_rounds):
        if r + 1 < n_rounds:
            qk(r + 1)
        softmax(r)
        pv(r)


def _attn_call(q, k, v, u, w):
    B, H, S, _ = q.shape
    head = lambda width: pl.BlockSpec((1, 1, S, width), lambda b, h: (b, h, 0, 0))
    return pl.pallas_call(
        _attn_body,
        grid=(B, H),
        in_specs=[
            head(QK_PAD), head(QK_PAD), head(DV), head(LANES),
            pl.BlockSpec((CONV_K, LANES), lambda b, h: (0, h)),
            pl.BlockSpec((1, LANES), lambda b, h: (0, h)),
        ],
        out_specs=[head(DV), head(LANES)],
        out_shape=[jax.ShapeDtypeStruct((B, H, S, DV), BF16),
                   jax.ShapeDtypeStruct((B, H, S, LANES), F32)],
        scratch_shapes=[pltpu.VMEM((S, 2 * DV), BF16),
                        pltpu.VMEM((2, 2, TQ, S), F32),
                        pltpu.VMEM((2, 2, TQ, S), BF16),
                        pltpu.VMEM((2, 2, TQ, 1), F32),
                        pltpu.VMEM((S + 2 * HALO + UBUF_EXTRA, LANES), F32),
                        pltpu.VMEM((2, S // 2 + HALO, LANES), U32)],
        compiler_params=_params(),
        name="mla_attention_conv",
    )(q, k, v, u, w["dw_packed"], w["dw_bias"])


def _merge_body(x_ref, attn_ref, c_ref, ng_ref, wg_ref, wo_ref, wpw_ref, wout_ref,
                lng_ref, lnb_ref, o_ref):
    x = x_ref[0]
    inv = lax.rsqrt(jnp.mean(x * x, axis=-1, keepdims=True) + EPS)
    h = (x * inv * ng_ref[...]).astype(BF16)

    ga = jax.nn.silu(jnp.dot(h, wg_ref[:, 0:D_MODEL], preferred_element_type=F32))
    attn = jnp.concatenate([attn_ref[0, hh] for hh in range(N_HEADS)], axis=1)
    ya_in = (attn.astype(F32) * ga).astype(BF16)
    y_a = jnp.dot(ya_in, wo_ref[...], preferred_element_type=F32)

    c = jnp.concatenate([c_ref[0, hh] for hh in range(N_HEADS)], axis=1)
    mu = jnp.mean(c, axis=-1, keepdims=True)
    cc = c - mu
    var = jnp.mean(cc * cc, axis=-1, keepdims=True)
    ln = cc * lax.rsqrt(var + EPS) * lng_ref[...] + lnb_ref[...]
    gc = jax.nn.silu(jnp.dot(h, wg_ref[:, D_MODEL:2 * D_MODEL], preferred_element_type=F32))
    yc_in = (jax.nn.silu(ln) * gc).astype(BF16)
    y_c = jnp.dot(yc_in, wpw_ref[...], preferred_element_type=F32)

    ma = jax.nn.sigmoid(jnp.dot(h, wg_ref[:, 2 * D_MODEL:3 * D_MODEL], preferred_element_type=F32))
    mc = jax.nn.sigmoid(jnp.dot(h, wg_ref[:, 3 * D_MODEL:4 * D_MODEL], preferred_element_type=F32))
    merged = (ma * y_a + mc * y_c).astype(BF16)
    o_ref[0] = x + jnp.dot(merged, wout_ref[...], preferred_element_type=F32)


def _merge_call(x, attn, c, w):
    B, S, D = x.shape
    tm = TM_MERGE
    tile = lambda width: pl.BlockSpec((1, tm, width), lambda b, i: (b, i, 0))
    heads = lambda width: pl.BlockSpec((1, N_HEADS, tm, width), lambda b, i: (b, 0, i, 0))
    in_specs = [
        tile(D), heads(DV), heads(LANES),
        _const_spec((1, D)),
        _const_spec((D, 4 * D_MODEL)),
        _const_spec((ATTN_WIDTH, D_MODEL)),
        _const_spec((CONV_W, D_MODEL)),
        _const_spec((D_MODEL, D_MODEL)),
        _const_spec((1, CONV_W)),
        _const_spec((1, CONV_W)),
    ]
    return pl.pallas_call(
        _merge_body,
        grid=(B, S // tm),
        in_specs=in_specs,
        out_specs=tile(D),
        out_shape=jax.ShapeDtypeStruct((B, S, D), F32),
        compiler_params=_params(),
        name="gate_merge",
    )(x, attn, c, w["norm_g"], w["w_gates"], w["w_o_attn"], w["w_pw2"], w["w_out"],
      w["conv_ln_g"], w["conv_ln_b"])


def _half_swap(t, sign):
    half = t.shape[-1] // 2
    return jnp.concatenate([sign * t[..., half:], t[..., :half]], axis=-1)


def _pack_bf16_pairs(t):
    half = lax.bitcast_convert_type(t.astype(BF16), jnp.uint16).astype(U32)
    return half | (half << 16)


def _prep_weights(norm_g, w_in, q_lora_g, w_uq, kv_lora_g, w_ukv, q_head_g, k_head_g, w_o_attn,
                  dw_kernel, dw_bias, conv_ln_g, conv_ln_b, w_pw2, w_out):
    offs = [0]
    for nsz in SPLITS:
        offs.append(offs[-1] + nsz)
    col = lambda g: w_in[:, offs[g]:offs[g + 1]]
    w_kr = col(2)
    w_conv = col(4)
    uq = w_uq.reshape(Q_RANK, N_HEADS, QK_DIM)
    uq_rope = uq[:, :, NOPE:]
    uq_tiles = jnp.concatenate([uq_rope, _half_swap(uq_rope, -1.0)], axis=-1)
    ukv = w_ukv.reshape(KV_RANK, N_HEADS, NOPE + DV)

    def head_gain(g):
        g_rope = g[NOPE:]
        return jnp.stack([g[:NOPE], jnp.concatenate([g_rope, _half_swap(g_rope, 1.0)])])

    return {
        "norm_g": norm_g.reshape(1, D_MODEL),
        "w_cq": col(0).astype(BF16),
        "w_kvin": jnp.concatenate([col(1), w_kr, _half_swap(w_kr, -1.0)], axis=1).astype(BF16),
        "w_a": w_conv[:, :CONV_W].astype(BF16),
        "w_b": w_conv[:, CONV_W:].astype(BF16),
        "q_lora_g": q_lora_g.reshape(1, Q_RANK),
        "w_uq": jnp.concatenate([uq[:, :, :NOPE].reshape(Q_RANK, ATTN_WIDTH),
                                 uq_tiles.reshape(Q_RANK, ATTN_WIDTH)], axis=1).astype(BF16),
        "kv_lora_g": kv_lora_g.reshape(1, KV_RANK),
        "w_ukv": jnp.concatenate([ukv[:, :, :NOPE].reshape(KV_RANK, ATTN_WIDTH),
                                  ukv[:, :, NOPE:].reshape(KV_RANK, ATTN_WIDTH)], axis=1).astype(BF16),
        "q_head_g": head_gain(q_head_g),
        "k_head_g": head_gain(k_head_g),
        "w_gates": jnp.concatenate([col(3), col(5), col(6), col(7)], axis=1).astype(BF16),
        "w_o_attn": w_o_attn.astype(BF16),
        "w_pw2": w_pw2.astype(BF16),
        "w_out": w_out.astype(BF16),
        "dw_packed": _pack_bf16_pairs(dw_kernel),
        "dw_bias": dw_bias.reshape(1, CONV_W),
        "conv_ln_g": conv_ln_g.reshape(1, CONV_W),
        "conv_ln_b": conv_ln_b.reshape(1, CONV_W),
    }


def _rope_table(seq_len):
    half = ROPE // 2
    inv_freq = 1.0 / (ROPE_THETA ** (jnp.arange(half, dtype=F32) / half))
    ang = jnp.arange(seq_len, dtype=F32)[:, None] * inv_freq[None, :]
    cos, sin = jnp.cos(ang), jnp.sin(ang)
    return jnp.concatenate([cos, cos, sin, sin], axis=1)


def _layer(x, w):
    cs = _rope_table(x.shape[1])
    q, k, v, u = _proj_call(x, cs, w)
    attn, c = _attn_call(q, k, v, u, w)
    return _merge_call(x, attn, c, w)


def kernel(x_prompt, x_sample, norm_g, w_in, q_lora_g, w_uq, kv_lora_g, w_ukv, q_head_g, k_head_g,
           w_o_attn, dw_kernel, dw_bias, conv_ln_g, conv_ln_b, w_pw2, w_out):
    depth = norm_g.shape[0]
    y_prompt, y_sample = x_prompt, x_sample
    for l in range(depth):
        w = _prep_weights(norm_g[l], w_in[l], q_lora_g[l], w_uq[l], kv_lora_g[l], w_ukv[l],
                          q_head_g[l], k_head_g[l], w_o_attn[l], dw_kernel[l], dw_bias[l],
                          conv_ln_g[l], conv_ln_b[l], w_pw2[l], w_out[l])
        y_prompt = _layer(y_prompt, w)
        y_sample = _layer(y_sample, w)
    return (y_prompt, y_sample)
```

```python
import math

import jax
import jax.numpy as jnp
from jax import lax
from jax.experimental import pallas as pl
from jax.experimental.pallas import tpu as pltpu

D_MODEL = 1024
N_HEADS = 8
NOPE = 128
ROPE = 64
QK_DIM = NOPE + ROPE
QK_PAD = 256
DV = 128
Q_RANK = 384
KV_RANK = 256
ATTN_WIDTH = N_HEADS * DV
CONV_W = D_MODEL
CONV_K = 31
CONV_PAD = (CONV_K - 1) // 2
ROPE_THETA = 10000.0
EPS = 1e-6
LANES = 128
HALO = 16
SPLITS = (Q_RANK, KV_RANK, ROPE, ATTN_WIDTH, 2 * CONV_W, CONV_W, D_MODEL, D_MODEL)

VMEM_LIMIT_BYTES = 56 * 1024 * 1024
TM_PROJ = 512
TM_MERGE = 1024
TQ = 256
HEADS_PER_STEP = 2
CONV_ROWS = 32
CONV_GROUP = 8
UBUF_EXTRA = 8

F32 = jnp.float32
BF16 = jnp.bfloat16
U32 = jnp.uint32

assert N_HEADS * LANES == CONV_W


def _const_spec(shape):
    return pl.BlockSpec(shape, lambda *_: (0,) * len(shape), pipeline_mode=pl.Buffered(1))


def _params():
    return pltpu.CompilerParams(dimension_semantics=("parallel", "parallel"),
                                vmem_limit_bytes=VMEM_LIMIT_BYTES)


def _proj_body(x_ref, cs_ref, ng_ref, wcq_ref, wkv_ref, wa_ref, wb_ref, qlg_ref, wuq_ref,
               kvlg_ref, wukv_ref, qg_ref, kg_ref, q_ref, k_ref, v_ref, u_ref):
    x = x_ref[0]
    inv = lax.rsqrt(jnp.mean(x * x, axis=-1, keepdims=True) + EPS)
    h = (x * inv * ng_ref[...]).astype(BF16)
    scale = math.log2(math.e) / math.sqrt(QK_DIM)
    cs = cs_ref[...]

    cq = jnp.dot(h, wcq_ref[...], preferred_element_type=F32)
    cq_inv = lax.rsqrt(jnp.mean(cq * cq, axis=-1, keepdims=True) + EPS)
    cqn = (cq * cq_inv * qlg_ref[...]).astype(BF16)
    qa = jnp.dot(cqn, wuq_ref[...], preferred_element_type=F32)
    q_cs = cs * qg_ref[1:2, :]
    for hh in range(N_HEADS):
        nope = qa[:, LANES * hh:LANES * (hh + 1)]
        rt = qa[:, ATTN_WIDTH + LANES * hh:ATTN_WIDTH + LANES * (hh + 1)]
        ss = jnp.sum(nope * nope + 0.5 * (rt * rt), axis=-1, keepdims=True)
        inv_h = lax.rsqrt(ss * (1.0 / QK_DIM) + EPS) * scale
        q_ref[0, hh, :, 0:LANES] = (nope * inv_h * qg_ref[0:1, :]).astype(BF16)
        t = rt * q_cs
        r = t + pltpu.roll(t, ROPE, 1)
        q_ref[0, hh, :, LANES:2 * LANES] = (r * inv_h).astype(BF16)

    kvin = jnp.dot(h, wkv_ref[...], preferred_element_type=F32)
    ckv = kvin[:, :KV_RANK]
    krt = kvin[:, KV_RANK:KV_RANK + LANES]
    ckv_inv = lax.rsqrt(jnp.mean(ckv * ckv, axis=-1, keepdims=True) + EPS)
    ckvn = (ckv * ckv_inv * kvlg_ref[...]).astype(BF16)
    kva = jnp.dot(ckvn, wukv_ref[...], preferred_element_type=F32)
    kr_sq = 0.5 * (krt * krt)
    tk = krt * (cs * kg_ref[1:2, :])
    rk = tk + pltpu.roll(tk, ROPE, 1)
    lane = lax.broadcasted_iota(jnp.int32, rk.shape, 1)
    rk = jnp.where(lane < ROPE, rk, 0.0)
    for hh in range(N_HEADS):
        kn = kva[:, LANES * hh:LANES * (hh + 1)]
        ss = jnp.sum(kn * kn + kr_sq, axis=-1, keepdims=True)
        inv_h = lax.rsqrt(ss * (1.0 / QK_DIM) + EPS)
        k_ref[0, hh, :, 0:LANES] = (kn * inv_h * kg_ref[0:1, :]).astype(BF16)
        k_ref[0, hh, :, LANES:2 * LANES] = (rk * inv_h).astype(BF16)
        v_ref[0, hh] = kva[:, ATTN_WIDTH + LANES * hh:ATTN_WIDTH + LANES * (hh + 1)].astype(BF16)

    a = jnp.dot(h, wa_ref[...], preferred_element_type=F32)
    b = jnp.dot(h, wb_ref[...], preferred_element_type=F32)
    u = (a * jax.nn.sigmoid(b)).astype(BF16)
    for hh in range(N_HEADS):
        u_ref[0, hh] = u[:, LANES * hh:LANES * (hh + 1)]


def _proj_call(x, cs, w):
    B, S, D = x.shape
    tm = TM_PROJ
    heads = lambda width: pl.BlockSpec((1, N_HEADS, tm, width), lambda b, i: (b, 0, i, 0))
    in_specs = [
        pl.BlockSpec((1, tm, D), lambda b, i: (b, i, 0)),
        pl.BlockSpec((tm, LANES), lambda b, i: (i, 0)),
        _const_spec((1, D)),
        _const_spec((D, Q_RANK)),
        _const_spec((D, KV_RANK + LANES)),
        _const_spec((D, CONV_W)),
        _const_spec((D, CONV_W)),
        _const_spec((1, Q_RANK)),
        _const_spec((Q_RANK, 2 * ATTN_WIDTH)),
        _const_spec((1, KV_RANK)),
        _const_spec((KV_RANK, 2 * ATTN_WIDTH)),
        _const_spec((2, LANES)),
        _const_spec((2, LANES)),
    ]
    out_specs = [heads(QK_PAD), heads(QK_PAD), heads(DV), heads(LANES)]
    out_shape = [
        jax.ShapeDtypeStruct((B, N_HEADS, S, QK_PAD), BF16),
        jax.ShapeDtypeStruct((B, N_HEADS, S, QK_PAD), BF16),
        jax.ShapeDtypeStruct((B, N_HEADS, S, DV), BF16),
        jax.ShapeDtypeStruct((B, N_HEADS, S, LANES), BF16),
    ]
    return pl.pallas_call(
        _proj_body,
        grid=(B, S // tm),
        in_specs=in_specs,
        out_specs=out_specs,
        out_shape=out_shape,
        compiler_params=_params(),
        name="mla_conv_proj",
    )(x, cs, w["norm_g"], w["w_cq"], w["w_kvin"], w["w_a"], w["w_b"], w["q_lora_g"], w["w_uq"],
      w["kv_lora_g"], w["w_ukv"], w["q_head_g"], w["k_head_g"])


def _attn_body(q_ref, k_ref, v_ref, u_ref, dwp_ref, dwb_ref, o_ref, c_ref,
               vaug_ref, s_ref, p_ref, m_ref, ubuf_ref, pair_ref):
    n_heads = k_ref.shape[1]
    S = k_ref.shape[2]
    tq = s_ref.shape[2]
    rounds_per_head = S // (2 * tq)
    n_words = S // 2 + HALO
    bits = lambda x: lax.bitcast_convert_type(x, U32)

    for hd in range(n_heads):
        vaug_ref[hd, :, 0:DV] = v_ref[0, hd]
        vaug_ref[hd, :, DV:2 * DV] = jnp.ones((S, DV), BF16)
        ubuf_ref[hd, 0:HALO, :] = jnp.zeros((HALO, LANES), F32)
        ubuf_ref[hd, HALO:HALO + S, :] = u_ref[0, hd].astype(F32)
        ubuf_ref[hd, HALO + S:2 * HALO + S + UBUF_EXTRA, :] = jnp.zeros((HALO + UBUF_EXTRA, LANES), F32)
        even = bits(ubuf_ref[hd, pl.ds(0, n_words, stride=2), :])
        odd = bits(ubuf_ref[hd, pl.ds(1, n_words, stride=2), :])
        even2 = bits(ubuf_ref[hd, pl.ds(2, n_words, stride=2), :])
        pair_ref[hd, 0] = (even >> 16) | odd
        pair_ref[hd, 1] = (odd >> 16) | even2

    chunks_per_block = S // CONV_ROWS // (2 * rounds_per_head)
    anchor_rows = tq // chunks_per_block
    assert chunks_per_block * 2 * rounds_per_head * CONV_ROWS == S and anchor_rows % 8 == 0

    def conv_chunk(hd, c):
        r0 = c * CONV_ROWS
        lanes = slice(LANES * hd, LANES * (hd + 1))

        def product(kk):
            start = r0 + (HALO - CONV_PAD) + kk
            win = pltpu.bitcast(pair_ref[hd, start % 2, pl.ds(start // 2, CONV_ROWS // 2), :], BF16)
            tap = jnp.broadcast_to(dwp_ref[kk:kk + 1, lanes], (CONV_ROWS // 2, LANES))
            return win * pltpu.bitcast(tap, BF16)

        def tree(terms):
            while len(terms) > 1:
                terms = [a + b for a, b in zip(terms[0::2], terms[1::2])] + terms[len(terms) & ~1:]
            return terms[0]

        out = dwb_ref[:, lanes]
        for g in range(0, CONV_K, CONV_GROUP):
            out = out + tree([product(kk) for kk in range(g, min(g + CONV_GROUP, CONV_K))]).astype(F32)
        c_ref[0, hd, r0:r0 + CONV_ROWS, :] = out
        return (bits(out[0:8, :]) >> 16) >> 16

    def split(rnd):
        return rnd // rounds_per_head, rnd % rounds_per_head

    def rows(r, j):
        return slice((2 * r + j) * tq, (2 * r + j + 1) * tq)

    def qk(rnd):
        hd, r = split(rnd)
        for j in range(2):
            s = lax.dot_general(q_ref[0, hd, rows(r, j), :], k_ref[0, hd],
                                (((1,), (1,)), ((), ())), preferred_element_type=F32)
            s_ref[rnd % 2, j] = s
            m_ref[rnd % 2, j] = jnp.max(s, axis=-1, keepdims=True)

    def softmax(rnd):
        hd, r = split(rnd)
        for j in range(2):
            for i in range(chunks_per_block):
                zero = conv_chunk(hd, (2 * r + j) * chunks_per_block + i)
                a0 = i * anchor_rows
                head = bits(s_ref[rnd % 2, j, a0:a0 + 8, 0:LANES])
                s_ref[rnd % 2, j, a0:a0 + 8, 0:LANES] = lax.bitcast_convert_type(head | zero, F32)
            s = s_ref[rnd % 2, j]
            p_ref[rnd % 2, j] = jnp.exp2(s - m_ref[rnd % 2, j]).astype(BF16)

    def pv(rnd):
        hd, r = split(rnd)
        os_ = [jnp.dot(p_ref[rnd % 2, j], vaug_ref[hd], preferred_element_type=F32) for j in range(2)]
        for j in range(2):
            o = os_[j]
            o_ref[0, hd, rows(r, j), :] = (o[:, 0:DV] / o[:, DV:2 * DV]).astype(BF16)

    n_rounds = n_heads * rounds_per_head
    qk(0)
    for rnd in range(n_rounds):
        if rnd + 1 < n_rounds:
            qk(rnd + 1)
        softmax(rnd)
        pv(rnd)


def _attn_call(q, k, v, u, w):
    B, H, S, _ = q.shape
    hps = HEADS_PER_STEP
    heads = lambda width: pl.BlockSpec((1, hps, S, width), lambda b, h: (b, h, 0, 0))
    return pl.pallas_call(
        _attn_body,
        grid=(B, H // hps),
        in_specs=[
            heads(QK_PAD), heads(QK_PAD), heads(DV), heads(LANES),
            pl.BlockSpec((CONV_K, hps * LANES), lambda b, h: (0, h)),
            pl.BlockSpec((1, hps * LANES), lambda b, h: (0, h)),
        ],
        out_specs=[heads(DV), heads(LANES)],
        out_shape=[jax.ShapeDtypeStruct((B, H, S, DV), BF16),
                   jax.ShapeDtypeStruct((B, H, S, LANES), F32)],
        scratch_shapes=[pltpu.VMEM((hps, S, 2 * DV), BF16),
                        pltpu.VMEM((2, 2, TQ, S), F32),
                        pltpu.VMEM((2, 2, TQ, S), BF16),
                        pltpu.VMEM((2, 2, TQ, 1), F32),
                        pltpu.VMEM((hps, S + 2 * HALO + UBUF_EXTRA, LANES), F32),
                        pltpu.VMEM((hps, 2, S // 2 + HALO, LANES), U32)],
        compiler_params=_params(),
        name="mla_attention_conv",
    )(q, k, v, u, w["dw_packed"], w["dw_bias"])


def _merge_body(x_ref, attn_ref, c_ref, ng_ref, wg_ref, wo_ref, wpw_ref, wout_ref,
                lng_ref, lnb_ref, o_ref):
    x = x_ref[0]
    inv = lax.rsqrt(jnp.mean(x * x, axis=-1, keepdims=True) + EPS)
    h = (x * inv * ng_ref[...]).astype(BF16)

    ga = jax.nn.silu(jnp.dot(h, wg_ref[:, 0:D_MODEL], preferred_element_type=F32))
    attn = jnp.concatenate([attn_ref[0, hh] for hh in range(N_HEADS)], axis=1)
    ya_in = (attn.astype(F32) * ga).astype(BF16)
    y_a = jnp.dot(ya_in, wo_ref[...], preferred_element_type=F32)

    c = jnp.concatenate([c_ref[0, hh] for hh in range(N_HEADS)], axis=1)
    mu = jnp.mean(c, axis=-1, keepdims=True)
    cc = c - mu
    var = jnp.mean(cc * cc, axis=-1, keepdims=True)
    ln = cc * lax.rsqrt(var + EPS) * lng_ref[...] + lnb_ref[...]
    gc = jax.nn.silu(jnp.dot(h, wg_ref[:, D_MODEL:2 * D_MODEL], preferred_element_type=F32))
    yc_in = (jax.nn.silu(ln) * gc).astype(BF16)
    y_c = jnp.dot(yc_in, wpw_ref[...], preferred_element_type=F32)

    ma = jax.nn.sigmoid(jnp.dot(h, wg_ref[:, 2 * D_MODEL:3 * D_MODEL], preferred_element_type=F32))
    mc = jax.nn.sigmoid(jnp.dot(h, wg_ref[:, 3 * D_MODEL:4 * D_MODEL], preferred_element_type=F32))
    merged = (ma * y_a + mc * y_c).astype(BF16)
    o_ref[0] = x + jnp.dot(merged, wout_ref[...], preferred_element_type=F32)


def _merge_call(x, attn, c, w):
    B, S, D = x.shape
    tm = TM_MERGE
    tile = lambda width: pl.BlockSpec((1, tm, width), lambda b, i: (b, i, 0))
    heads = lambda width: pl.BlockSpec((1, N_HEADS, tm, width), lambda b, i: (b, 0, i, 0))
    in_specs = [
        tile(D), heads(DV), heads(LANES),
        _const_spec((1, D)),
        _const_spec((D, 4 * D_MODEL)),
        _const_spec((ATTN_WIDTH, D_MODEL)),
        _const_spec((CONV_W, D_MODEL)),
        _const_spec((D_MODEL, D_MODEL)),
        _const_spec((1, CONV_W)),
        _const_spec((1, CONV_W)),
    ]
    return pl.pallas_call(
        _merge_body,
        grid=(B, S // tm),
        in_specs=in_specs,
        out_specs=tile(D),
        out_shape=jax.ShapeDtypeStruct((B, S, D), F32),
        compiler_params=_params(),
        name="gate_merge",
    )(x, attn, c, w["norm_g"], w["w_gates"], w["w_o_attn"], w["w_pw2"], w["w_out"],
      w["conv_ln_g"], w["conv_ln_b"])


def _half_swap(t, sign):
    half = t.shape[-1] // 2
    return jnp.concatenate([sign * t[..., half:], t[..., :half]], axis=-1)


def _pack_bf16_pairs(t):
    half = lax.bitcast_convert_type(t.astype(BF16), jnp.uint16).astype(U32)
    return half | (half << 16)


def _prep_weights(norm_g, w_in, q_lora_g, w_uq, kv_lora_g, w_ukv, q_head_g, k_head_g, w_o_attn,
                  dw_kernel, dw_bias, conv_ln_g, conv_ln_b, w_pw2, w_out):
    offs = [0]
    for nsz in SPLITS:
        offs.append(offs[-1] + nsz)
    col = lambda g: w_in[:, offs[g]:offs[g + 1]]
    w_kr = col(2)
    w_conv = col(4)
    uq = w_uq.reshape(Q_RANK, N_HEADS, QK_DIM)
    uq_rope = uq[:, :, NOPE:]
    uq_tiles = jnp.concatenate([uq_rope, _half_swap(uq_rope, -1.0)], axis=-1)
    ukv = w_ukv.reshape(KV_RANK, N_HEADS, NOPE + DV)

    def head_gain(g):
        g_rope = g[NOPE:]
        return jnp.stack([g[:NOPE], jnp.concatenate([g_rope, _half_swap(g_rope, 1.0)])])

    return {
        "norm_g": norm_g.reshape(1, D_MODEL),
        "w_cq": col(0).astype(BF16),
        "w_kvin": jnp.concatenate([col(1), w_kr, _half_swap(w_kr, -1.0)], axis=1).astype(BF16),
        "w_a": w_conv[:, :CONV_W].astype(BF16),
        "w_b": w_conv[:, CONV_W:].astype(BF16),
        "q_lora_g": q_lora_g.reshape(1, Q_RANK),
        "w_uq": jnp.concatenate([uq[:, :, :NOPE].reshape(Q_RANK, ATTN_WIDTH),
                                 uq_tiles.reshape(Q_RANK, ATTN_WIDTH)], axis=1).astype(BF16),
        "kv_lora_g": kv_lora_g.reshape(1, KV_RANK),
        "w_ukv": jnp.concatenate([ukv[:, :, :NOPE].reshape(KV_RANK, ATTN_WIDTH),
                                  ukv[:, :, NOPE:].reshape(KV_RANK, ATTN_WIDTH)], axis=1).astype(BF16),
        "q_head_g": head_gain(q_head_g),
        "k_head_g": head_gain(k_head_g),
        "w_gates": jnp.concatenate([col(3), col(5), col(6), col(7)], axis=1).astype(BF16),
        "w_o_attn": w_o_attn.astype(BF16),
        "w_pw2": w_pw2.astype(BF16),
        "w_out": w_out.astype(BF16),
        "dw_packed": _pack_bf16_pairs(dw_kernel),
        "dw_bias": dw_bias.reshape(1, CONV_W),
        "conv_ln_g": conv_ln_g.reshape(1, CONV_W),
        "conv_ln_b": conv_ln_b.reshape(1, CONV_W),
    }


def _rope_table(seq_len):
    half = ROPE // 2
    inv_freq = 1.0 / (ROPE_THETA ** (jnp.arange(half, dtype=F32) / half))
    ang = jnp.arange(seq_len, dtype=F32)[:, None] * inv_freq[None, :]
    cos, sin = jnp.cos(ang), jnp.sin(ang)
    return jnp.concatenate([cos, cos, sin, sin], axis=1)


def _layer(x, w):
    cs = _rope_table(x.shape[1])
    q, k, v, u = _proj_call(x, cs, w)
    attn, c = _attn_call(q, k, v, u, w)
    return _merge_call(x, attn, c, w)


def kernel(x_prompt, x_sample, norm_g, w_in, q_lora_g, w_uq, kv_lora_g, w_ukv, q_head_g, k_head_g,
           w_o_attn, dw_kernel, dw_bias, conv_ln_g, conv_ln_b, w_pw2, w_out):
    depth = norm_g.shape[0]
    y_prompt, y_sample = x_prompt, x_sample
    for l in range(depth):
        w = _prep_weights(norm_g[l], w_in[l], q_lora_g[l], w_uq[l], kv_lora_g[l], w_ukv[l],
                          q_head_g[l], k_head_g[l], w_o_attn[l], dw_kernel[l], dw_bias[l],
                          conv_ln_g[l], conv_ln_b[l], w_pw2[l], w_out[l])
        y_prompt = _layer(y_prompt, w)
        y_sample = _layer(y_sample, w)
    return (y_prompt, y_sample)
```

```python
import math

import jax
import jax.numpy as jnp
from jax import lax
from jax.experimental import pallas as pl
from jax.experimental.pallas import tpu as pltpu

D_MODEL = 1024
N_HEADS = 8
NOPE = 128
ROPE = 64
QK_DIM = NOPE + ROPE
QK_PAD = 256
DV = 128
Q_RANK = 384
KV_RANK = 256
ATTN_WIDTH = N_HEADS * DV
CONV_W = D_MODEL
CONV_K = 31
CONV_PAD = (CONV_K - 1) // 2
ROPE_THETA = 10000.0
EPS = 1e-6
LANES = 128
HALO = 16
SPLITS = (Q_RANK, KV_RANK, ROPE, ATTN_WIDTH, 2 * CONV_W, CONV_W, D_MODEL, D_MODEL)

VMEM_LIMIT_BYTES = 60 * 1024 * 1024
TM_PROJ = 1024
TM_MERGE = 1024
TQ = 256
HEADS_PER_STEP = 2
CONV_ROWS = 32
CONV_GROUP = 8
UBUF_EXTRA = 8

F32 = jnp.float32
BF16 = jnp.bfloat16
U32 = jnp.uint32

assert N_HEADS * LANES == CONV_W


def _const_spec(shape):
    return pl.BlockSpec(shape, lambda *_: (0,) * len(shape), pipeline_mode=pl.Buffered(1))


def _params():
    return pltpu.CompilerParams(dimension_semantics=("parallel", "parallel"),
                                vmem_limit_bytes=VMEM_LIMIT_BYTES)


def _proj_body(x_ref, cs_ref, ng_ref, wcq_ref, wkv_ref, wa_ref, wb_ref, qlg_ref, wuq_ref,
               kvlg_ref, wukv_ref, qg_ref, kg_ref, q_ref, k_ref, v_ref, u_ref):
    x = x_ref[0]
    inv = lax.rsqrt(jnp.mean(x * x, axis=-1, keepdims=True) + EPS)
    h = (x * inv * ng_ref[...]).astype(BF16)
    scale = math.log2(math.e) / math.sqrt(QK_DIM)
    cs = cs_ref[...]

    cq = jnp.dot(h, wcq_ref[...], preferred_element_type=F32)
    cq_inv = lax.rsqrt(jnp.mean(cq * cq, axis=-1, keepdims=True) + EPS)
    cqn = (cq * cq_inv * qlg_ref[...]).astype(BF16)
    qa = jnp.dot(cqn, wuq_ref[...], preferred_element_type=F32)
    q_cs = cs * qg_ref[1:2, :]
    for hh in range(N_HEADS):
        nope = qa[:, LANES * hh:LANES * (hh + 1)]
        rt = qa[:, ATTN_WIDTH + LANES * hh:ATTN_WIDTH + LANES * (hh + 1)]
        ss = jnp.sum(nope * nope + 0.5 * (rt * rt), axis=-1, keepdims=True)
        inv_h = lax.rsqrt(ss * (1.0 / QK_DIM) + EPS) * scale
        q_ref[0, hh, :, 0:LANES] = (nope * inv_h * qg_ref[0:1, :]).astype(BF16)
        t = rt * q_cs
        r = t + pltpu.roll(t, ROPE, 1)
        q_ref[0, hh, :, LANES:2 * LANES] = (r * inv_h).astype(BF16)

    kvin = jnp.dot(h, wkv_ref[...], preferred_element_type=F32)
    ckv = kvin[:, :KV_RANK]
    krt = kvin[:, KV_RANK:KV_RANK + LANES]
    ckv_inv = lax.rsqrt(jnp.mean(ckv * ckv, axis=-1, keepdims=True) + EPS)
    ckvn = (ckv * ckv_inv * kvlg_ref[...]).astype(BF16)
    kva = jnp.dot(ckvn, wukv_ref[...], preferred_element_type=F32)
    kr_sq = 0.5 * (krt * krt)
    tk = krt * (cs * kg_ref[1:2, :])
    rk = tk + pltpu.roll(tk, ROPE, 1)
    lane = lax.broadcasted_iota(jnp.int32, rk.shape, 1)
    rk = jnp.where(lane < ROPE, rk, 0.0)
    for hh in range(N_HEADS):
        kn = kva[:, LANES * hh:LANES * (hh + 1)]
        ss = jnp.sum(kn * kn + kr_sq, axis=-1, keepdims=True)
        inv_h = lax.rsqrt(ss * (1.0 / QK_DIM) + EPS)
        k_ref[0, hh, :, 0:LANES] = (kn * inv_h * kg_ref[0:1, :]).astype(BF16)
        k_ref[0, hh, :, LANES:2 * LANES] = (rk * inv_h).astype(BF16)
        v_ref[0, hh] = kva[:, ATTN_WIDTH + LANES * hh:ATTN_WIDTH + LANES * (hh + 1)].astype(BF16)

    a = jnp.dot(h, wa_ref[...], preferred_element_type=F32)
    b = jnp.dot(h, wb_ref[...], preferred_element_type=F32)
    u = (a * jax.nn.sigmoid(b)).astype(BF16)
    for hh in range(N_HEADS):
        u_ref[0, hh] = u[:, LANES * hh:LANES * (hh + 1)]


def _proj_call(x, cs, w):
    B, S, D = x.shape
    tm = TM_PROJ
    heads = lambda width: pl.BlockSpec((1, N_HEADS, tm, width), lambda b, i: (b, 0, i, 0))
    in_specs = [
        pl.BlockSpec((1, tm, D), lambda b, i: (b, i, 0)),
        pl.BlockSpec((tm, LANES), lambda b, i: (i, 0)),
        _const_spec((1, D)),
        _const_spec((D, Q_RANK)),
        _const_spec((D, KV_RANK + LANES)),
        _const_spec((D, CONV_W)),
        _const_spec((D, CONV_W)),
        _const_spec((1, Q_RANK)),
        _const_spec((Q_RANK, 2 * ATTN_WIDTH)),
        _const_spec((1, KV_RANK)),
        _const_spec((KV_RANK, 2 * ATTN_WIDTH)),
        _const_spec((2, LANES)),
        _const_spec((2, LANES)),
    ]
    out_specs = [heads(QK_PAD), heads(QK_PAD), heads(DV), heads(LANES)]
    out_shape = [
        jax.ShapeDtypeStruct((B, N_HEADS, S, QK_PAD), BF16),
        jax.ShapeDtypeStruct((B, N_HEADS, S, QK_PAD), BF16),
        jax.ShapeDtypeStruct((B, N_HEADS, S, DV), BF16),
        jax.ShapeDtypeStruct((B, N_HEADS, S, LANES), BF16),
    ]
    return pl.pallas_call(
        _proj_body,
        grid=(B, S // tm),
        in_specs=in_specs,
        out_specs=out_specs,
        out_shape=out_shape,
        compiler_params=_params(),
        name="mla_conv_proj",
    )(x, cs, w["norm_g"], w["w_cq"], w["w_kvin"], w["w_a"], w["w_b"], w["q_lora_g"], w["w_uq"],
      w["kv_lora_g"], w["w_ukv"], w["q_head_g"], w["k_head_g"])


def _attn_body(q_ref, k_ref, v_ref, u_ref, dwp_ref, dwb_ref, o_ref, c_ref,
               vaug_ref, s_ref, p_ref, m_ref, ubuf_ref, pair_ref):
    n_heads = k_ref.shape[1]
    S = k_ref.shape[2]
    tq = s_ref.shape[2]
    rounds_per_head = S // (2 * tq)
    n_words = S // 2 + HALO
    bits = lambda x: lax.bitcast_convert_type(x, U32)

    for hd in range(n_heads):
        vaug_ref[hd, :, 0:DV] = v_ref[0, hd]
        vaug_ref[hd, :, DV:2 * DV] = jnp.ones((S, DV), BF16)
        ubuf_ref[hd, 0:HALO, :] = jnp.zeros((HALO, LANES), F32)
        ubuf_ref[hd, HALO:HALO + S, :] = u_ref[0, hd].astype(F32)
        ubuf_ref[hd, HALO + S:2 * HALO + S + UBUF_EXTRA, :] = jnp.zeros((HALO + UBUF_EXTRA, LANES), F32)
        even = bits(ubuf_ref[hd, pl.ds(0, n_words, stride=2), :])
        odd = bits(ubuf_ref[hd, pl.ds(1, n_words, stride=2), :])
        even2 = bits(ubuf_ref[hd, pl.ds(2, n_words, stride=2), :])
        pair_ref[hd, 0] = (even >> 16) | odd
        pair_ref[hd, 1] = (odd >> 16) | even2

    chunks_per_block = S // CONV_ROWS // (2 * rounds_per_head)
    anchor_rows = tq // chunks_per_block
    assert chunks_per_block * 2 * rounds_per_head * CONV_ROWS == S and anchor_rows % 8 == 0

    def conv_chunk(hd, c):
        r0 = c * CONV_ROWS
        lanes = slice(LANES * hd, LANES * (hd + 1))

        def product(kk):
            start = r0 + (HALO - CONV_PAD) + kk
            win = pltpu.bitcast(pair_ref[hd, start % 2, pl.ds(start // 2, CONV_ROWS // 2), :], BF16)
            tap = jnp.broadcast_to(dwp_ref[kk:kk + 1, lanes], (CONV_ROWS // 2, LANES))
            return win * pltpu.bitcast(tap, BF16)

        def tree(terms):
            while len(terms) > 1:
                terms = [a + b for a, b in zip(terms[0::2], terms[1::2])] + terms[len(terms) & ~1:]
            return terms[0]

        out = dwb_ref[:, lanes]
        for g in range(0, CONV_K, CONV_GROUP):
            out = out + tree([product(kk) for kk in range(g, min(g + CONV_GROUP, CONV_K))]).astype(F32)
        c_ref[0, hd, r0:r0 + CONV_ROWS, :] = out
        return (bits(out[0:8, :]) >> 16) >> 16

    def split(rnd):
        return rnd // rounds_per_head, rnd % rounds_per_head

    def rows(r, j):
        return slice((2 * r + j) * tq, (2 * r + j + 1) * tq)

    def qk(rnd):
        hd, r = split(rnd)
        for j in range(2):
            s = lax.dot_general(q_ref[0, hd, rows(r, j), :], k_ref[0, hd],
                                (((1,), (1,)), ((), ())), preferred_element_type=F32)
            s_ref[rnd % 2, j] = s
            m_ref[rnd % 2, j] = jnp.max(s, axis=-1, keepdims=True)

    def softmax(rnd):
        hd, r = split(rnd)
        for j in range(2):
            for i in range(chunks_per_block):
                zero = conv_chunk(hd, (2 * r + j) * chunks_per_block + i)
                a0 = i * anchor_rows
                head = bits(s_ref[rnd % 2, j, a0:a0 + 8, 0:LANES])
                s_ref[rnd % 2, j, a0:a0 + 8, 0:LANES] = lax.bitcast_convert_type(head | zero, F32)
            s = s_ref[rnd % 2, j]
            p_ref[rnd % 2, j] = jnp.exp2(s - m_ref[rnd % 2, j]).astype(BF16)

    def pv(rnd):
        hd, r = split(rnd)
        os_ = [jnp.dot(p_ref[rnd % 2, j], vaug_ref[hd], preferred_element_type=F32) for j in range(2)]
        for j in range(2):
            o = os_[j]
            o_ref[0, hd, rows(r, j), :] = (o[:, 0:DV] / o[:, DV:2 * DV]).astype(BF16)

    n_rounds = n_heads * rounds_per_head
    qk(0)
    for rnd in range(n_rounds):
        if rnd + 1 < n_rounds:
            qk(rnd + 1)
        softmax(rnd)
        pv(rnd)


def _attn_call(q, k, v, u, w):
    B, H, S, _ = q.shape
    hps = HEADS_PER_STEP
    heads = lambda width: pl.BlockSpec((1, hps, S, width), lambda b, h: (b, h, 0, 0))
    return pl.pallas_call(
        _attn_body,
        grid=(B, H // hps),
        in_specs=[
            heads(QK_PAD), heads(QK_PAD), heads(DV), heads(LANES),
            pl.BlockSpec((CONV_K, hps * LANES), lambda b, h: (0, h)),
            pl.BlockSpec((1, hps * LANES), lambda b, h: (0, h)),
        ],
        out_specs=[heads(DV), heads(LANES)],
        out_shape=[jax.ShapeDtypeStruct((B, H, S, DV), BF16),
                   jax.ShapeDtypeStruct((B, H, S, LANES), F32)],
        scratch_shapes=[pltpu.VMEM((hps, S, 2 * DV), BF16),
                        pltpu.VMEM((2, 2, TQ, S), F32),
                        pltpu.VMEM((2, 2, TQ, S), BF16),
                        pltpu.VMEM((2, 2, TQ, 1), F32),
                        pltpu.VMEM((hps, S + 2 * HALO + UBUF_EXTRA, LANES), F32),
                        pltpu.VMEM((hps, 2, S // 2 + HALO, LANES), U32)],
        compiler_params=_params(),
        name="mla_attention_conv",
    )(q, k, v, u, w["dw_packed"], w["dw_bias"])


def _merge_body(x_ref, attn_ref, c_ref, ng_ref, wg_ref, wo_ref, wpw_ref, wout_ref,
                lng_ref, lnb_ref, o_ref):
    x = x_ref[0]
    inv = lax.rsqrt(jnp.mean(x * x, axis=-1, keepdims=True) + EPS)
    h = (x * inv * ng_ref[...]).astype(BF16)

    ga = jax.nn.silu(jnp.dot(h, wg_ref[:, 0:D_MODEL], preferred_element_type=F32))
    attn = jnp.concatenate([attn_ref[0, hh] for hh in range(N_HEADS)], axis=1)
    ya_in = (attn.astype(F32) * ga).astype(BF16)
    y_a = jnp.dot(ya_in, wo_ref[...], preferred_element_type=F32)

    c = jnp.concatenate([c_ref[0, hh] for hh in range(N_HEADS)], axis=1)
    mu = jnp.mean(c, axis=-1, keepdims=True)
    cc = c - mu
    var = jnp.mean(cc * cc, axis=-1, keepdims=True)
    ln = cc * lax.rsqrt(var + EPS) * lng_ref[...] + lnb_ref[...]
    gc = jax.nn.silu(jnp.dot(h, wg_ref[:, D_MODEL:2 * D_MODEL], preferred_element_type=F32))
    yc_in = (jax.nn.silu(ln) * gc).astype(BF16)
    y_c = jnp.dot(yc_in, wpw_ref[...], preferred_element_type=F32)

    ma = jax.nn.sigmoid(jnp.dot(h, wg_ref[:, 2 * D_MODEL:3 * D_MODEL], preferred_element_type=F32))
    mc = jax.nn.sigmoid(jnp.dot(h, wg_ref[:, 3 * D_MODEL:4 * D_MODEL], preferred_element_type=F32))
    merged = (ma * y_a + mc * y_c).astype(BF16)
    o_ref[0] = x + jnp.dot(merged, wout_ref[...], preferred_element_type=F32)


def _merge_call(x, attn, c, w):
    B, S, D = x.shape
    tm = TM_MERGE
    tile = lambda width: pl.BlockSpec((1, tm, width), lambda b, i: (b, i, 0))
    heads = lambda width: pl.BlockSpec((1, N_HEADS, tm, width), lambda b, i: (b, 0, i, 0))
    in_specs = [
        tile(D), heads(DV), heads(LANES),
        _const_spec((1, D)),
        _const_spec((D, 4 * D_MODEL)),
        _const_spec((ATTN_WIDTH, D_MODEL)),
        _const_spec((CONV_W, D_MODEL)),
        _const_spec((D_MODEL, D_MODEL)),
        _const_spec((1, CONV_W)),
        _const_spec((1, CONV_W)),
    ]
    return pl.pallas_call(
        _merge_body,
        grid=(B, S // tm),
        in_specs=in_specs,
        out_specs=tile(D),
        out_shape=jax.ShapeDtypeStruct((B, S, D), F32),
        compiler_params=_params(),
        name="gate_merge",
    )(x, attn, c, w["norm_g"], w["w_gates"], w["w_o_attn"], w["w_pw2"], w["w_out"],
      w["conv_ln_g"], w["conv_ln_b"])


def _half_swap(t, sign):
    half = t.shape[-1] // 2
    return jnp.concatenate([sign * t[..., half:], t[..., :half]], axis=-1)


def _pack_bf16_pairs(t):
    half = lax.bitcast_convert_type(t.astype(BF16), jnp.uint16).astype(U32)
    return half | (half << 16)


def _prep_weights(norm_g, w_in, q_lora_g, w_uq, kv_lora_g, w_ukv, q_head_g, k_head_g, w_o_attn,
                  dw_kernel, dw_bias, conv_ln_g, conv_ln_b, w_pw2, w_out):
    offs = [0]
    for nsz in SPLITS:
        offs.append(offs[-1] + nsz)
    col = lambda g: w_in[:, offs[g]:offs[g + 1]]
    w_kr = col(2)
    w_conv = col(4)
    uq = w_uq.reshape(Q_RANK, N_HEADS, QK_DIM)
    uq_rope = uq[:, :, NOPE:]
    uq_tiles = jnp.concatenate([uq_rope, _half_swap(uq_rope, -1.0)], axis=-1)
    ukv = w_ukv.reshape(KV_RANK, N_HEADS, NOPE + DV)

    def head_gain(g):
        g_rope = g[NOPE:]
        return jnp.stack([g[:NOPE], jnp.concatenate([g_rope, _half_swap(g_rope, 1.0)])])

    return {
        "norm_g": norm_g.reshape(1, D_MODEL),
        "w_cq": col(0).astype(BF16),
        "w_kvin": jnp.concatenate([col(1), w_kr, _half_swap(w_kr, -1.0)], axis=1).astype(BF16),
        "w_a": w_conv[:, :CONV_W].astype(BF16),
        "w_b": w_conv[:, CONV_W:].astype(BF16),
        "q_lora_g": q_lora_g.reshape(1, Q_RANK),
        "w_uq": jnp.concatenate([uq[:, :, :NOPE].reshape(Q_RANK, ATTN_WIDTH),
                                 uq_tiles.reshape(Q_RANK, ATTN_WIDTH)], axis=1).astype(BF16),
        "kv_lora_g": kv_lora_g.reshape(1, KV_RANK),
        "w_ukv": jnp.concatenate([ukv[:, :, :NOPE].reshape(KV_RANK, ATTN_WIDTH),
                                  ukv[:, :, NOPE:].reshape(KV_RANK, ATTN_WIDTH)], axis=1).astype(BF16),
        "q_head_g": head_gain(q_head_g),
        "k_head_g": head_gain(k_head_g),
        "w_gates": jnp.concatenate([col(3), col(5), col(6), col(7)], axis=1).astype(BF16),
        "w_o_attn": w_o_attn.astype(BF16),
        "w_pw2": w_pw2.astype(BF16),
        "w_out": w_out.astype(BF16),
        "dw_packed": _pack_bf16_pairs(dw_kernel),
        "dw_bias": dw_bias.reshape(1, CONV_W),
        "conv_ln_g": conv_ln_g.reshape(1, CONV_W),
        "conv_ln_b": conv_ln_b.reshape(1, CONV_W),
    }


def _rope_table(seq_len):
    half = ROPE // 2
    inv_freq = 1.0 / (ROPE_THETA ** (jnp.arange(half, dtype=F32) / half))
    ang = jnp.arange(seq_len, dtype=F32)[:, None] * inv_freq[None, :]
    cos, sin = jnp.cos(ang), jnp.sin(ang)
    return jnp.concatenate([cos, cos, sin, sin], axis=1)


def _layer(x, w):
    cs = _rope_table(x.shape[1])
    q, k, v, u = _proj_call(x, cs, w)
    attn, c = _attn_call(q, k, v, u, w)
    return _merge_call(x, attn, c, w)


def kernel(x_prompt, x_sample, norm_g, w_in, q_lora_g, w_uq, kv_lora_g, w_ukv, q_head_g, k_head_g,
           w_o_attn, dw_kernel, dw_bias, conv_ln_g, conv_ln_b, w_pw2, w_out):
    depth = norm_g.shape[0]
    y_prompt, y_sample = x_prompt, x_sample
    for l in range(depth):
        w = _prep_weights(norm_g[l], w_in[l], q_lora_g[l], w_uq[l], kv_lora_g[l], w_ukv[l],
                          q_head_g[l], k_head_g[l], w_o_attn[l], dw_kernel[l], dw_bias[l],
                          conv_ln_g[l], conv_ln_b[l], w_pw2[l], w_out[l])
        y_prompt = _layer(y_prompt, w)
        y_sample = _layer(y_sample, w)
    return (y_prompt, y_sample)
```

```python
import math

import jax
import jax.numpy as jnp
from jax import lax
from jax.experimental import pallas as pl
from jax.experimental.pallas import tpu as pltpu

D_MODEL = 1024
N_HEADS = 8
NOPE = 128
ROPE = 64
QK_DIM = NOPE + ROPE
QK_PAD = 256
DV = 128
Q_RANK = 384
KV_RANK = 256
ATTN_WIDTH = N_HEADS * DV
CONV_W = D_MODEL
CONV_K = 31
CONV_PAD = (CONV_K - 1) // 2
ROPE_THETA = 10000.0
EPS = 1e-6
LANES = 128
HALO = 16
SPLITS = (Q_RANK, KV_RANK, ROPE, ATTN_WIDTH, 2 * CONV_W, CONV_W, D_MODEL, D_MODEL)

VMEM_LIMIT_BYTES = 60 * 1024 * 1024
TM_PROJ = 1024
TM_MERGE = 1024
TQ = 256
WSPLIT_STEPS = 8
HEADS_PER_STEP = 2
CONV_ROWS = 32
CONV_GROUP = 8
UBUF_EXTRA = 8

F32 = jnp.float32
BF16 = jnp.bfloat16
U32 = jnp.uint32

assert N_HEADS * LANES == CONV_W


def _const_spec(shape):
    return pl.BlockSpec(shape, lambda *_: (0,) * len(shape), pipeline_mode=pl.Buffered(1))


def _params():
    return pltpu.CompilerParams(dimension_semantics=("parallel", "parallel"),
                                vmem_limit_bytes=VMEM_LIMIT_BYTES)


def _proj_body(x_ref, cs_ref, ng_ref, wcq_ref, wkv_ref, wa_ref, wb_ref, qlg_ref, wuq_ref,
               kvlg_ref, wukv_ref, qg_ref, kg_ref, q_ref, k_ref, v_ref, u_ref):
    x = x_ref[0]
    inv = lax.rsqrt(jnp.mean(x * x, axis=-1, keepdims=True) + EPS)
    h = (x * inv * ng_ref[...]).astype(BF16)
    scale = math.log2(math.e) / math.sqrt(QK_DIM)
    cs = cs_ref[...]

    cq = jnp.dot(h, wcq_ref[...], preferred_element_type=F32)
    cq_inv = lax.rsqrt(jnp.mean(cq * cq, axis=-1, keepdims=True) + EPS)
    cqn = (cq * cq_inv * qlg_ref[...]).astype(BF16)
    qa = jnp.dot(cqn, wuq_ref[...], preferred_element_type=F32)
    q_cs = cs * qg_ref[1:2, :]
    for hh in range(N_HEADS):
        nope = qa[:, LANES * hh:LANES * (hh + 1)]
        rt = qa[:, ATTN_WIDTH + LANES * hh:ATTN_WIDTH + LANES * (hh + 1)]
        ss = jnp.sum(nope * nope + 0.5 * (rt * rt), axis=-1, keepdims=True)
        inv_h = lax.rsqrt(ss * (1.0 / QK_DIM) + EPS) * scale
        q_ref[0, hh, :, 0:LANES] = (nope * inv_h * qg_ref[0:1, :]).astype(BF16)
        t = rt * q_cs
        r = t + pltpu.roll(t, ROPE, 1)
        q_ref[0, hh, :, LANES:2 * LANES] = (r * inv_h).astype(BF16)

    kvin = jnp.dot(h, wkv_ref[...], preferred_element_type=F32)
    ckv = kvin[:, :KV_RANK]
    krt = kvin[:, KV_RANK:KV_RANK + LANES]
    ckv_inv = lax.rsqrt(jnp.mean(ckv * ckv, axis=-1, keepdims=True) + EPS)
    ckvn = (ckv * ckv_inv * kvlg_ref[...]).astype(BF16)
    kva = jnp.dot(ckvn, wukv_ref[...], preferred_element_type=F32)
    kr_sq = 0.5 * (krt * krt)
    tk = krt * (cs * kg_ref[1:2, :])
    rk = tk + pltpu.roll(tk, ROPE, 1)
    lane = lax.broadcasted_iota(jnp.int32, rk.shape, 1)
    rk = jnp.where(lane < ROPE, rk, 0.0)
    for hh in range(N_HEADS):
        kn = kva[:, LANES * hh:LANES * (hh + 1)]
        ss = jnp.sum(kn * kn + kr_sq, axis=-1, keepdims=True)
        inv_h = lax.rsqrt(ss * (1.0 / QK_DIM) + EPS)
        k_ref[0, hh, :, 0:LANES] = (kn * inv_h * kg_ref[0:1, :]).astype(BF16)
        k_ref[0, hh, :, LANES:2 * LANES] = (rk * inv_h).astype(BF16)
        v_ref[0, hh] = kva[:, ATTN_WIDTH + LANES * hh:ATTN_WIDTH + LANES * (hh + 1)].astype(BF16)

    a = jnp.dot(h, wa_ref[...], preferred_element_type=F32)
    b = jnp.dot(h, wb_ref[...], preferred_element_type=F32)
    u = (a * jax.nn.sigmoid(b)).astype(BF16)
    for hh in range(N_HEADS):
        u_ref[0, hh] = u[:, LANES * hh:LANES * (hh + 1)]


def _proj_call(x, cs, w):
    B, S, D = x.shape
    tm = TM_PROJ
    heads = lambda width: pl.BlockSpec((1, N_HEADS, tm, width), lambda b, i: (b, 0, i, 0))
    in_specs = [
        pl.BlockSpec((1, tm, D), lambda b, i: (b, i, 0)),
        pl.BlockSpec((tm, LANES), lambda b, i: (i, 0)),
        _const_spec((1, D)),
        _const_spec((D, Q_RANK)),
        _const_spec((D, KV_RANK + LANES)),
        _const_spec((D, CONV_W)),
        _const_spec((D, CONV_W)),
        _const_spec((1, Q_RANK)),
        _const_spec((Q_RANK, 2 * ATTN_WIDTH)),
        _const_spec((1, KV_RANK)),
        _const_spec((KV_RANK, 2 * ATTN_WIDTH)),
        _const_spec((2, LANES)),
        _const_spec((2, LANES)),
    ]
    out_specs = [heads(QK_PAD), heads(QK_PAD), heads(DV), heads(LANES)]
    out_shape = [
        jax.ShapeDtypeStruct((B, N_HEADS, S, QK_PAD), BF16),
        jax.ShapeDtypeStruct((B, N_HEADS, S, QK_PAD), BF16),
        jax.ShapeDtypeStruct((B, N_HEADS, S, DV), BF16),
        jax.ShapeDtypeStruct((B, N_HEADS, S, LANES), BF16),
    ]
    return pl.pallas_call(
        _proj_body,
        grid=(B, S // tm),
        in_specs=in_specs,
        out_specs=out_specs,
        out_shape=out_shape,
        compiler_params=_params(),
        name="mla_conv_proj",
    )(x, cs, w["norm_g"], w["w_cq"], w["w_kvin"], w["w_a"], w["w_b"], w["q_lora_g"], w["w_uq"],
      w["kv_lora_g"], w["w_ukv"], w["q_head_g"], w["k_head_g"])


def _attn_body(q_ref, k_ref, v_ref, u_ref, dwp_ref, dwb_ref, o_ref, c_ref,
               vaug_ref, s_ref, p_ref, m_ref, ubuf_ref, pair_ref):
    n_heads = k_ref.shape[1]
    S = k_ref.shape[2]
    tq = s_ref.shape[2]
    rounds_per_head = S // (2 * tq)
    n_words = S // 2 + HALO
    bits = lambda x: lax.bitcast_convert_type(x, U32)

    for hd in range(n_heads):
        vaug_ref[hd, :, 0:DV] = v_ref[0, hd]
        vaug_ref[hd, :, DV:2 * DV] = jnp.ones((S, DV), BF16)
        ubuf_ref[hd, 0:HALO, :] = jnp.zeros((HALO, LANES), F32)
        ubuf_ref[hd, HALO:HALO + S, :] = u_ref[0, hd].astype(F32)
        ubuf_ref[hd, HALO + S:2 * HALO + S + UBUF_EXTRA, :] = jnp.zeros((HALO + UBUF_EXTRA, LANES), F32)
        even = bits(ubuf_ref[hd, pl.ds(0, n_words, stride=2), :])
        odd = bits(ubuf_ref[hd, pl.ds(1, n_words, stride=2), :])
        even2 = bits(ubuf_ref[hd, pl.ds(2, n_words, stride=2), :])
        pair_ref[hd, 0] = (even >> 16) | odd
        pair_ref[hd, 1] = (odd >> 16) | even2

    chunks_per_block = S // CONV_ROWS // (2 * rounds_per_head)
    anchor_rows = tq // chunks_per_block
    assert chunks_per_block * 2 * rounds_per_head * CONV_ROWS == S and anchor_rows % 8 == 0

    def conv_chunk(hd, c):
        r0 = c * CONV_ROWS
        lanes = slice(LANES * hd, LANES * (hd + 1))

        def product(kk):
            start = r0 + (HALO - CONV_PAD) + kk
            win = pltpu.bitcast(pair_ref[hd, start % 2, pl.ds(start // 2, CONV_ROWS // 2), :], BF16)
            tap = jnp.broadcast_to(dwp_ref[kk:kk + 1, lanes], (CONV_ROWS // 2, LANES))
            return win * pltpu.bitcast(tap, BF16)

        def tree(terms):
            while len(terms) > 1:
                terms = [a + b for a, b in zip(terms[0::2], terms[1::2])] + terms[len(terms) & ~1:]
            return terms[0]

        out = dwb_ref[:, lanes]
        for g in range(0, CONV_K, CONV_GROUP):
            out = out + tree([product(kk) for kk in range(g, min(g + CONV_GROUP, CONV_K))]).astype(F32)
        c_ref[0, hd, r0:r0 + CONV_ROWS, :] = out
        return (bits(out[0:8, :]) >> 16) >> 16

    def split(rnd):
        return rnd // rounds_per_head, rnd % rounds_per_head

    def rows(r, j):
        return slice((2 * r + j) * tq, (2 * r + j + 1) * tq)

    def qk(rnd):
        hd, r = split(rnd)
        for j in range(2):
            s = lax.dot_general(q_ref[0, hd, rows(r, j), :], k_ref[0, hd],
                                (((1,), (1,)), ((), ())), preferred_element_type=F32)
            s_ref[rnd % 2, j] = s
            m_ref[rnd % 2, j] = jnp.max(s, axis=-1, keepdims=True)

    def softmax(rnd):
        hd, r = split(rnd)
        for j in range(2):
            for i in range(chunks_per_block):
                zero = conv_chunk(hd, (2 * r + j) * chunks_per_block + i)
                a0 = i * anchor_rows
                head = bits(s_ref[rnd % 2, j, a0:a0 + 8, 0:LANES])
                s_ref[rnd % 2, j, a0:a0 + 8, 0:LANES] = lax.bitcast_convert_type(head | zero, F32)
            s = s_ref[rnd % 2, j]
            p_ref[rnd % 2, j] = jnp.exp2(s - m_ref[rnd % 2, j]).astype(BF16)

    def pv(rnd):
        hd, r = split(rnd)
        os_ = [jnp.dot(p_ref[rnd % 2, j], vaug_ref[hd], preferred_element_type=F32) for j in range(2)]
        for j in range(2):
            o = os_[j]
            o_ref[0, hd, rows(r, j), :] = (o[:, 0:DV] / o[:, DV:2 * DV]).astype(BF16)

    n_rounds = n_heads * rounds_per_head
    qk(0)
    for rnd in range(n_rounds):
        if rnd + 1 < n_rounds:
            qk(rnd + 1)
        softmax(rnd)
        pv(rnd)


def _attn_call(q, k, v, u, w):
    B, H, S, _ = q.shape
    hps = HEADS_PER_STEP
    heads = lambda width: pl.BlockSpec((1, hps, S, width), lambda b, h: (b, h, 0, 0))
    return pl.pallas_call(
        _attn_body,
        grid=(B, H // hps),
        in_specs=[
            heads(QK_PAD), heads(QK_PAD), heads(DV), heads(LANES),
            pl.BlockSpec((CONV_K, hps * LANES), lambda b, h: (0, h)),
            pl.BlockSpec((1, hps * LANES), lambda b, h: (0, h)),
        ],
        out_specs=[heads(DV), heads(LANES)],
        out_shape=[jax.ShapeDtypeStruct((B, H, S, DV), BF16),
                   jax.ShapeDtypeStruct((B, H, S, LANES), F32)],
        scratch_shapes=[pltpu.VMEM((hps, S, 2 * DV), BF16),
                        pltpu.VMEM((2, 2, TQ, S), F32),
                        pltpu.VMEM((2, 2, TQ, S), BF16),
                        pltpu.VMEM((2, 2, TQ, 1), F32),
                        pltpu.VMEM((hps, S + 2 * HALO + UBUF_EXTRA, LANES), F32),
                        pltpu.VMEM((hps, 2, S // 2 + HALO, LANES), U32)],
        compiler_params=_params(),
        name="mla_attention_conv",
    )(q, k, v, u, w["dw_packed"], w["dw_bias"])


def _merge_body(x_ref, attn_ref, c_ref, ng_ref, wg_ref, wo_ref, wpw_ref, wout_ref,
                lng_ref, lnb_ref, o_ref):
    x = x_ref[0]
    inv = lax.rsqrt(jnp.mean(x * x, axis=-1, keepdims=True) + EPS)
    h = (x * inv * ng_ref[...]).astype(BF16)

    ga = jax.nn.silu(jnp.dot(h, wg_ref[:, 0:D_MODEL], preferred_element_type=F32))
    attn = jnp.concatenate([attn_ref[0, hh] for hh in range(N_HEADS)], axis=1)
    ya_in = (attn.astype(F32) * ga).astype(BF16)
    y_a = jnp.dot(ya_in, wo_ref[...], preferred_element_type=F32)

    c = jnp.concatenate([c_ref[0, hh] for hh in range(N_HEADS)], axis=1)
    mu = jnp.mean(c, axis=-1, keepdims=True)
    cc = c - mu
    var = jnp.mean(cc * cc, axis=-1, keepdims=True)
    ln = cc * lax.rsqrt(var + EPS) * lng_ref[...] + lnb_ref[...]
    gc = jax.nn.silu(jnp.dot(h, wg_ref[:, D_MODEL:2 * D_MODEL], preferred_element_type=F32))
    yc_in = (jax.nn.silu(ln) * gc).astype(BF16)
    y_c = jnp.dot(yc_in, wpw_ref[...], preferred_element_type=F32)

    ma = jax.nn.sigmoid(jnp.dot(h, wg_ref[:, 2 * D_MODEL:3 * D_MODEL], preferred_element_type=F32))
    mc = jax.nn.sigmoid(jnp.dot(h, wg_ref[:, 3 * D_MODEL:4 * D_MODEL], preferred_element_type=F32))
    merged = (ma * y_a + mc * y_c).astype(BF16)
    o_ref[0] = x + jnp.dot(merged, wout_ref[...], preferred_element_type=F32)


def _merge_call(x, attn, c, w):
    B, S, D = x.shape
    tm = TM_MERGE
    tile = lambda width: pl.BlockSpec((1, tm, width), lambda b, i: (b, i, 0))
    heads = lambda width: pl.BlockSpec((1, N_HEADS, tm, width), lambda b, i: (b, 0, i, 0))
    in_specs = [
        tile(D), heads(DV), heads(LANES),
        _const_spec((1, D)),
        _const_spec((D, 4 * D_MODEL)),
        _const_spec((ATTN_WIDTH, D_MODEL)),
        _const_spec((CONV_W, D_MODEL)),
        _const_spec((D_MODEL, D_MODEL)),
        _const_spec((1, CONV_W)),
        _const_spec((1, CONV_W)),
    ]
    return pl.pallas_call(
        _merge_body,
        grid=(B, S // tm),
        in_specs=in_specs,
        out_specs=tile(D),
        out_shape=jax.ShapeDtypeStruct((B, S, D), F32),
        compiler_params=_params(),
        name="gate_merge",
    )(x, attn, c, w["norm_g"], w["w_gates"], w["w_o_attn"], w["w_pw2"], w["w_out"],
      w["conv_ln_g"], w["conv_ln_b"])


def _wsplit_body(w_ref, wcq_ref, wa_ref, wb_ref, wg_ref):
    offs = _split_offsets()
    cols = lambda g: slice(offs[g], offs[g + 1])
    wcq_ref[...] = w_ref[:, cols(0)].astype(BF16)
    wa_ref[...] = w_ref[:, offs[4]:offs[4] + CONV_W].astype(BF16)
    wb_ref[...] = w_ref[:, offs[4] + CONV_W:offs[5]].astype(BF16)
    wg_ref[:, 0:D_MODEL] = w_ref[:, cols(3)].astype(BF16)
    wg_ref[:, D_MODEL:4 * D_MODEL] = w_ref[:, offs[5]:offs[8]].astype(BF16)


def _wsplit_call(w_in):
    rows = D_MODEL // WSPLIT_STEPS
    out_cols = (Q_RANK, CONV_W, CONV_W, 4 * D_MODEL)
    return pl.pallas_call(
        _wsplit_body,
        grid=(WSPLIT_STEPS,),
        in_specs=[pl.BlockSpec((rows, w_in.shape[1]), lambda i: (i, 0))],
        out_specs=[pl.BlockSpec((rows, n), lambda i: (i, 0)) for n in out_cols],
        out_shape=[jax.ShapeDtypeStruct((D_MODEL, n), BF16) for n in out_cols],
        compiler_params=pltpu.CompilerParams(dimension_semantics=("parallel",),
                                             vmem_limit_bytes=VMEM_LIMIT_BYTES),
        name="w_in_split",
    )(w_in)


def _half_swap(t, sign):
    half = t.shape[-1] // 2
    return jnp.concatenate([sign * t[..., half:], t[..., :half]], axis=-1)


def _pack_bf16_pairs(t):
    half = lax.bitcast_convert_type(t.astype(BF16), jnp.uint16).astype(U32)
    return half | (half << 16)


def _split_offsets():
    offs = [0]
    for nsz in SPLITS:
        offs.append(offs[-1] + nsz)
    return offs


def _prep_weights(norm_g, w_in, q_lora_g, w_uq, kv_lora_g, w_ukv, q_head_g, k_head_g, w_o_attn,
                  dw_kernel, dw_bias, conv_ln_g, conv_ln_b, w_pw2, w_out):
    offs = _split_offsets()
    col = lambda g: w_in[:, offs[g]:offs[g + 1]]
    w_kr = col(2)
    w_cq, w_a, w_b, w_gates = _wsplit_call(w_in)
    uq = w_uq.reshape(Q_RANK, N_HEADS, QK_DIM)
    uq_rope = uq[:, :, NOPE:]
    uq_tiles = jnp.concatenate([uq_rope, _half_swap(uq_rope, -1.0)], axis=-1)
    ukv = w_ukv.reshape(KV_RANK, N_HEADS, NOPE + DV)

    def head_gain(g):
        g_rope = g[NOPE:]
        return jnp.stack([g[:NOPE], jnp.concatenate([g_rope, _half_swap(g_rope, 1.0)])])

    return {
        "norm_g": norm_g.reshape(1, D_MODEL),
        "w_cq": w_cq,
        "w_kvin": jnp.concatenate([col(1), w_kr, _half_swap(w_kr, -1.0)], axis=1).astype(BF16),
        "w_a": w_a,
        "w_b": w_b,
        "q_lora_g": q_lora_g.reshape(1, Q_RANK),
        "w_uq": jnp.concatenate([uq[:, :, :NOPE].reshape(Q_RANK, ATTN_WIDTH),
                                 uq_tiles.reshape(Q_RANK, ATTN_WIDTH)], axis=1).astype(BF16),
        "kv_lora_g": kv_lora_g.reshape(1, KV_RANK),
        "w_ukv": jnp.concatenate([ukv[:, :, :NOPE].reshape(KV_RANK, ATTN_WIDTH),
                                  ukv[:, :, NOPE:].reshape(KV_RANK, ATTN_WIDTH)], axis=1).astype(BF16),
        "q_head_g": head_gain(q_head_g),
        "k_head_g": head_gain(k_head_g),
        "w_gates": w_gates,
        "w_o_attn": w_o_attn.astype(BF16),
        "w_pw2": w_pw2.astype(BF16),
        "w_out": w_out.astype(BF16),
        "dw_packed": _pack_bf16_pairs(dw_kernel),
        "dw_bias": dw_bias.reshape(1, CONV_W),
        "conv_ln_g": conv_ln_g.reshape(1, CONV_W),
        "conv_ln_b": conv_ln_b.reshape(1, CONV_W),
    }


def _rope_table(seq_len):
    half = ROPE // 2
    inv_freq = 1.0 / (ROPE_THETA ** (jnp.arange(half, dtype=F32) / half))
    ang = jnp.arange(seq_len, dtype=F32)[:, None] * inv_freq[None, :]
    cos, sin = jnp.cos(ang), jnp.sin(ang)
    return jnp.concatenate([cos, cos, sin, sin], axis=1)


def _layer(x, w):
    cs = _rope_table(x.shape[1])
    q, k, v, u = _proj_call(x, cs, w)
    attn, c = _attn_call(q, k, v, u, w)
    return _merge_call(x, attn, c, w)


def kernel(x_prompt, x_sample, norm_g, w_in, q_lora_g, w_uq, kv_lora_g, w_ukv, q_head_g, k_head_g,
           w_o_attn, dw_kernel, dw_bias, conv_ln_g, conv_ln_b, w_pw2, w_out):
    depth = norm_g.shape[0]
    y_prompt, y_sample = x_prompt, x_sample
    for l in range(depth):
        w = _prep_weights(norm_g[l], w_in[l], q_lora_g[l], w_uq[l], kv_lora_g[l], w_ukv[l],
                          q_head_g[l], k_head_g[l], w_o_attn[l], dw_kernel[l], dw_bias[l],
                          conv_ln_g[l], conv_ln_b[l], w_pw2[l], w_out[l])
        y_prompt = _layer(y_prompt, w)
        y_sample = _layer(y_sample, w)
    return (y_prompt, y_sample)
```

```python
import math

import jax
import jax.numpy as jnp
from jax import lax
from jax.experimental import pallas as pl
from jax.experimental.pallas import tpu as pltpu

D_MODEL = 1024
N_HEADS = 8
NOPE = 128
ROPE = 64
QK_DIM = NOPE + ROPE
QK_PAD = 256
DV = 128
Q_RANK = 384
KV_RANK = 256
ATTN_WIDTH = N_HEADS * DV
CONV_W = D_MODEL
CONV_K = 31
CONV_PAD = (CONV_K - 1) // 2
ROPE_THETA = 10000.0
EPS = 1e-6
LANES = 128
HALO = 16
SPLITS = (Q_RANK, KV_RANK, ROPE, ATTN_WIDTH, 2 * CONV_W, CONV_W, D_MODEL, D_MODEL)

VMEM_LIMIT_BYTES = 60 * 1024 * 1024
TM_PROJ = 1024
TM_MERGE = 1024
TQ = 512
WSPLIT_STEPS = 8
HEADS_PER_STEP = 2
CONV_ROWS = 32
CONV_GROUP = 8
UBUF_EXTRA = 8

F32 = jnp.float32
BF16 = jnp.bfloat16
U32 = jnp.uint32

assert N_HEADS * LANES == CONV_W


def _const_spec(shape):
    return pl.BlockSpec(shape, lambda *_: (0,) * len(shape), pipeline_mode=pl.Buffered(1))


def _params():
    return pltpu.CompilerParams(dimension_semantics=("parallel", "parallel"),
                                vmem_limit_bytes=VMEM_LIMIT_BYTES)


def _proj_body(x_ref, cs_ref, ng_ref, wcq_ref, wkv_ref, wa_ref, wb_ref, qlg_ref, wuq_ref,
               kvlg_ref, wukv_ref, qg_ref, kg_ref, q_ref, k_ref, v_ref, u_ref):
    x = x_ref[0]
    inv = lax.rsqrt(jnp.mean(x * x, axis=-1, keepdims=True) + EPS)
    h = (x * inv * ng_ref[...]).astype(BF16)
    scale = math.log2(math.e) / math.sqrt(QK_DIM)
    cs = cs_ref[...]

    cq = jnp.dot(h, wcq_ref[...], preferred_element_type=F32)
    cq_inv = lax.rsqrt(jnp.mean(cq * cq, axis=-1, keepdims=True) + EPS)
    cqn = (cq * cq_inv * qlg_ref[...]).astype(BF16)
    qa = jnp.dot(cqn, wuq_ref[...], preferred_element_type=F32)
    q_cs = cs * qg_ref[1:2, :]
    for hh in range(N_HEADS):
        nope = qa[:, LANES * hh:LANES * (hh + 1)]
        rt = qa[:, ATTN_WIDTH + LANES * hh:ATTN_WIDTH + LANES * (hh + 1)]
        ss = jnp.sum(nope * nope + 0.5 * (rt * rt), axis=-1, keepdims=True)
        inv_h = lax.rsqrt(ss * (1.0 / QK_DIM) + EPS) * scale
        q_ref[0, hh, :, 0:LANES] = (nope * inv_h * qg_ref[0:1, :]).astype(BF16)
        t = rt * q_cs
        r = t + pltpu.roll(t, ROPE, 1)
        q_ref[0, hh, :, LANES:2 * LANES] = (r * inv_h).astype(BF16)

    kvin = jnp.dot(h, wkv_ref[...], preferred_element_type=F32)
    ckv = kvin[:, :KV_RANK]
    krt = kvin[:, KV_RANK:KV_RANK + LANES]
    ckv_inv = lax.rsqrt(jnp.mean(ckv * ckv, axis=-1, keepdims=True) + EPS)
    ckvn = (ckv * ckv_inv * kvlg_ref[...]).astype(BF16)
    kva = jnp.dot(ckvn, wukv_ref[...], preferred_element_type=F32)
    kr_sq = 0.5 * (krt * krt)
    tk = krt * (cs * kg_ref[1:2, :])
    rk = tk + pltpu.roll(tk, ROPE, 1)
    lane = lax.broadcasted_iota(jnp.int32, rk.shape, 1)
    rk = jnp.where(lane < ROPE, rk, 0.0)
    for hh in range(N_HEADS):
        kn = kva[:, LANES * hh:LANES * (hh + 1)]
        ss = jnp.sum(kn * kn + kr_sq, axis=-1, keepdims=True)
        inv_h = lax.rsqrt(ss * (1.0 / QK_DIM) + EPS)
        k_ref[0, hh, :, 0:LANES] = (kn * inv_h * kg_ref[0:1, :]).astype(BF16)
        k_ref[0, hh, :, LANES:2 * LANES] = (rk * inv_h).astype(BF16)
        v_ref[0, hh] = kva[:, ATTN_WIDTH + LANES * hh:ATTN_WIDTH + LANES * (hh + 1)].astype(BF16)

    a = jnp.dot(h, wa_ref[...], preferred_element_type=F32)
    b = jnp.dot(h, wb_ref[...], preferred_element_type=F32)
    u = (a * jax.nn.sigmoid(b)).astype(BF16)
    for hh in range(N_HEADS):
        u_ref[0, hh] = u[:, LANES * hh:LANES * (hh + 1)]


def _proj_call(x, cs, w):
    B, S, D = x.shape
    tm = TM_PROJ
    heads = lambda width: pl.BlockSpec((1, N_HEADS, tm, width), lambda b, i: (b, 0, i, 0))
    in_specs = [
        pl.BlockSpec((1, tm, D), lambda b, i: (b, i, 0)),
        pl.BlockSpec((tm, LANES), lambda b, i: (i, 0)),
        _const_spec((1, D)),
        _const_spec((D, Q_RANK)),
        _const_spec((D, KV_RANK + LANES)),
        _const_spec((D, CONV_W)),
        _const_spec((D, CONV_W)),
        _const_spec((1, Q_RANK)),
        _const_spec((Q_RANK, 2 * ATTN_WIDTH)),
        _const_spec((1, KV_RANK)),
        _const_spec((KV_RANK, 2 * ATTN_WIDTH)),
        _const_spec((2, LANES)),
        _const_spec((2, LANES)),
    ]
    out_specs = [heads(QK_PAD), heads(QK_PAD), heads(DV), heads(LANES)]
    out_shape = [
        jax.ShapeDtypeStruct((B, N_HEADS, S, QK_PAD), BF16),
        jax.ShapeDtypeStruct((B, N_HEADS, S, QK_PAD), BF16),
        jax.ShapeDtypeStruct((B, N_HEADS, S, DV), BF16),
        jax.ShapeDtypeStruct((B, N_HEADS, S, LANES), BF16),
    ]
    return pl.pallas_call(
        _proj_body,
        grid=(B, S // tm),
        in_specs=in_specs,
        out_specs=out_specs,
        out_shape=out_shape,
        compiler_params=_params(),
        name="mla_conv_proj",
    )(x, cs, w["norm_g"], w["w_cq"], w["w_kvin"], w["w_a"], w["w_b"], w["q_lora_g"], w["w_uq"],
      w["kv_lora_g"], w["w_ukv"], w["q_head_g"], w["k_head_g"])


def _attn_body(q_ref, k_ref, v_ref, u_ref, dwp_ref, dwb_ref, o_ref, c_ref,
               vaug_ref, s_ref, p_ref, m_ref, ubuf_ref, pair_ref):
    n_heads = k_ref.shape[1]
    S = k_ref.shape[2]
    tq = s_ref.shape[2]
    rounds_per_head = S // (2 * tq)
    n_words = S // 2 + HALO
    bits = lambda x: lax.bitcast_convert_type(x, U32)

    for hd in range(n_heads):
        vaug_ref[hd, :, 0:DV] = v_ref[0, hd]
        vaug_ref[hd, :, DV:2 * DV] = jnp.ones((S, DV), BF16)
        ubuf_ref[hd, 0:HALO, :] = jnp.zeros((HALO, LANES), F32)
        ubuf_ref[hd, HALO:HALO + S, :] = u_ref[0, hd].astype(F32)
        ubuf_ref[hd, HALO + S:2 * HALO + S + UBUF_EXTRA, :] = jnp.zeros((HALO + UBUF_EXTRA, LANES), F32)
        even = bits(ubuf_ref[hd, pl.ds(0, n_words, stride=2), :])
        odd = bits(ubuf_ref[hd, pl.ds(1, n_words, stride=2), :])
        even2 = bits(ubuf_ref[hd, pl.ds(2, n_words, stride=2), :])
        pair_ref[hd, 0] = (even >> 16) | odd
        pair_ref[hd, 1] = (odd >> 16) | even2

    chunks_per_block = S // CONV_ROWS // (2 * rounds_per_head)
    anchor_rows = tq // chunks_per_block
    assert chunks_per_block * 2 * rounds_per_head * CONV_ROWS == S and anchor_rows % 8 == 0

    def conv_chunk(hd, c):
        r0 = c * CONV_ROWS
        lanes = slice(LANES * hd, LANES * (hd + 1))

        def product(kk):
            start = r0 + (HALO - CONV_PAD) + kk
            win = pltpu.bitcast(pair_ref[hd, start % 2, pl.ds(start // 2, CONV_ROWS // 2), :], BF16)
            tap = jnp.broadcast_to(dwp_ref[kk:kk + 1, lanes], (CONV_ROWS // 2, LANES))
            return win * pltpu.bitcast(tap, BF16)

        def tree(terms):
            while len(terms) > 1:
                terms = [a + b for a, b in zip(terms[0::2], terms[1::2])] + terms[len(terms) & ~1:]
            return terms[0]

        out = dwb_ref[:, lanes]
        for g in range(0, CONV_K, CONV_GROUP):
            out = out + tree([product(kk) for kk in range(g, min(g + CONV_GROUP, CONV_K))]).astype(F32)
        c_ref[0, hd, r0:r0 + CONV_ROWS, :] = out
        return (bits(out[0:8, :]) >> 16) >> 16

    def split(rnd):
        return rnd // rounds_per_head, rnd % rounds_per_head

    def rows(r, j):
        return slice((2 * r + j) * tq, (2 * r + j + 1) * tq)

    def qk(rnd):
        hd, r = split(rnd)
        for j in range(2):
            s = lax.dot_general(q_ref[0, hd, rows(r, j), :], k_ref[0, hd],
                                (((1,), (1,)), ((), ())), preferred_element_type=F32)
            s_ref[rnd % 2, j] = s
            m_ref[rnd % 2, j] = jnp.max(s, axis=-1, keepdims=True)

    def softmax(rnd):
        hd, r = split(rnd)
        for j in range(2):
            for i in range(chunks_per_block):
                zero = conv_chunk(hd, (2 * r + j) * chunks_per_block + i)
                a0 = i * anchor_rows
                head = bits(s_ref[rnd % 2, j, a0:a0 + 8, 0:LANES])
                s_ref[rnd % 2, j, a0:a0 + 8, 0:LANES] = lax.bitcast_convert_type(head | zero, F32)
            s = s_ref[rnd % 2, j]
            p_ref[rnd % 2, j] = jnp.exp2(s - m_ref[rnd % 2, j]).astype(BF16)

    def pv(rnd):
        hd, r = split(rnd)
        os_ = [jnp.dot(p_ref[rnd % 2, j], vaug_ref[hd], preferred_element_type=F32) for j in range(2)]
        for j in range(2):
            o = os_[j]
            o_ref[0, hd, rows(r, j), :] = (o[:, 0:DV] / o[:, DV:2 * DV]).astype(BF16)

    n_rounds = n_heads * rounds_per_head
    qk(0)
    for rnd in range(n_rounds):
        if rnd + 1 < n_rounds:
            qk(rnd + 1)
        softmax(rnd)
        pv(rnd)


def _attn_call(q, k, v, u, w):
    B, H, S, _ = q.shape
    hps = HEADS_PER_STEP
    heads = lambda width: pl.BlockSpec((1, hps, S, width), lambda b, h: (b, h, 0, 0))
    return pl.pallas_call(
        _attn_body,
        grid=(B, H // hps),
        in_specs=[
            heads(QK_PAD), heads(QK_PAD), heads(DV), heads(LANES),
            pl.BlockSpec((CONV_K, hps * LANES), lambda b, h: (0, h)),
            pl.BlockSpec((1, hps * LANES), lambda b, h: (0, h)),
        ],
        out_specs=[heads(DV), heads(LANES)],
        out_shape=[jax.ShapeDtypeStruct((B, H, S, DV), BF16),
                   jax.ShapeDtypeStruct((B, H, S, LANES), F32)],
        scratch_shapes=[pltpu.VMEM((hps, S, 2 * DV), BF16),
                        pltpu.VMEM((2, 2, TQ, S), F32),
                        pltpu.VMEM((2, 2, TQ, S), BF16),
                        pltpu.VMEM((2, 2, TQ, 1), F32),
                        pltpu.VMEM((hps, S + 2 * HALO + UBUF_EXTRA, LANES), F32),
                        pltpu.VMEM((hps, 2, S // 2 + HALO, LANES), U32)],
        compiler_params=_params(),
        name="mla_attention_conv",
    )(q, k, v, u, w["dw_packed"], w["dw_bias"])


def _merge_body(x_ref, attn_ref, c_ref, ng_ref, wg_ref, wo_ref, wpw_ref, wout_ref,
                lng_ref, lnb_ref, o_ref):
    x = x_ref[0]
    inv = lax.rsqrt(jnp.mean(x * x, axis=-1, keepdims=True) + EPS)
    h = (x * inv * ng_ref[...]).astype(BF16)

    ga = jax.nn.silu(jnp.dot(h, wg_ref[:, 0:D_MODEL], preferred_element_type=F32))
    attn = jnp.concatenate([attn_ref[0, hh] for hh in range(N_HEADS)], axis=1)
    ya_in = (attn.astype(F32) * ga).astype(BF16)
    y_a = jnp.dot(ya_in, wo_ref[...], preferred_element_type=F32)

    c = jnp.concatenate([c_ref[0, hh] for hh in range(N_HEADS)], axis=1)
    mu = jnp.mean(c, axis=-1, keepdims=True)
    cc = c - mu
    var = jnp.mean(cc * cc, axis=-1, keepdims=True)
    ln = cc * lax.rsqrt(var + EPS) * lng_ref[...] + lnb_ref[...]
    gc = jax.nn.silu(jnp.dot(h, wg_ref[:, D_MODEL:2 * D_MODEL], preferred_element_type=F32))
    yc_in = (jax.nn.silu(ln) * gc).astype(BF16)
    y_c = jnp.dot(yc_in, wpw_ref[...], preferred_element_type=F32)

    ma = jax.nn.sigmoid(jnp.dot(h, wg_ref[:, 2 * D_MODEL:3 * D_MODEL], preferred_element_type=F32))
    mc = jax.nn.sigmoid(jnp.dot(h, wg_ref[:, 3 * D_MODEL:4 * D_MODEL], preferred_element_type=F32))
    merged = (ma * y_a + mc * y_c).astype(BF16)
    o_ref[0] = x + jnp.dot(merged, wout_ref[...], preferred_element_type=F32)


def _merge_call(x, attn, c, w):
    B, S, D = x.shape
    tm = TM_MERGE
    tile = lambda width: pl.BlockSpec((1, tm, width), lambda b, i: (b, i, 0))
    heads = lambda width: pl.BlockSpec((1, N_HEADS, tm, width), lambda b, i: (b, 0, i, 0))
    in_specs = [
        tile(D), heads(DV), heads(LANES),
        _const_spec((1, D)),
        _const_spec((D, 4 * D_MODEL)),
        _const_spec((ATTN_WIDTH, D_MODEL)),
        _const_spec((CONV_W, D_MODEL)),
        _const_spec((D_MODEL, D_MODEL)),
        _const_spec((1, CONV_W)),
        _const_spec((1, CONV_W)),
    ]
    return pl.pallas_call(
        _merge_body,
        grid=(B, S // tm),
        in_specs=in_specs,
        out_specs=tile(D),
        out_shape=jax.ShapeDtypeStruct((B, S, D), F32),
        compiler_params=_params(),
        name="gate_merge",
    )(x, attn, c, w["norm_g"], w["w_gates"], w["w_o_attn"], w["w_pw2"], w["w_out"],
      w["conv_ln_g"], w["conv_ln_b"])


def _wsplit_body(w_ref, wcq_ref, wa_ref, wb_ref, wg_ref):
    offs = _split_offsets()
    cols = lambda g: slice(offs[g], offs[g + 1])
    wcq_ref[...] = w_ref[:, cols(0)].astype(BF16)
    wa_ref[...] = w_ref[:, offs[4]:offs[4] + CONV_W].astype(BF16)
    wb_ref[...] = w_ref[:, offs[4] + CONV_W:offs[5]].astype(BF16)
    wg_ref[:, 0:D_MODEL] = w_ref[:, cols(3)].astype(BF16)
    wg_ref[:, D_MODEL:4 * D_MODEL] = w_ref[:, offs[5]:offs[8]].astype(BF16)


def _wsplit_call(w_in):
    rows = D_MODEL // WSPLIT_STEPS
    out_cols = (Q_RANK, CONV_W, CONV_W, 4 * D_MODEL)
    return pl.pallas_call(
        _wsplit_body,
        grid=(WSPLIT_STEPS,),
        in_specs=[pl.BlockSpec((rows, w_in.shape[1]), lambda i: (i, 0))],
        out_specs=[pl.BlockSpec((rows, n), lambda i: (i, 0)) for n in out_cols],
        out_shape=[jax.ShapeDtypeStruct((D_MODEL, n), BF16) for n in out_cols],
        compiler_params=pltpu.CompilerParams(dimension_semantics=("parallel",),
                                             vmem_limit_bytes=VMEM_LIMIT_BYTES),
        name="w_in_split",
    )(w_in)


def _half_swap(t, sign):
    half = t.shape[-1] // 2
    return jnp.concatenate([sign * t[..., half:], t[..., :half]], axis=-1)


def _pack_bf16_pairs(t):
    half = lax.bitcast_convert_type(t.astype(BF16), jnp.uint16).astype(U32)
    return half | (half << 16)


def _split_offsets():
    offs = [0]
    for nsz in SPLITS:
        offs.append(offs[-1] + nsz)
    return offs


def _prep_weights(norm_g, w_in, q_lora_g, w_uq, kv_lora_g, w_ukv, q_head_g, k_head_g, w_o_attn,
                  dw_kernel, dw_bias, conv_ln_g, conv_ln_b, w_pw2, w_out):
    offs = _split_offsets()
    col = lambda g: w_in[:, offs[g]:offs[g + 1]]
    w_kr = col(2)
    w_cq, w_a, w_b, w_gates = _wsplit_call(w_in)
    uq = w_uq.reshape(Q_RANK, N_HEADS, QK_DIM)
    uq_rope = uq[:, :, NOPE:]
    uq_tiles = jnp.concatenate([uq_rope, _half_swap(uq_rope, -1.0)], axis=-1)
    ukv = w_ukv.reshape(KV_RANK, N_HEADS, NOPE + DV)

    def head_gain(g):
        g_rope = g[NOPE:]
        return jnp.stack([g[:NOPE], jnp.concatenate([g_rope, _half_swap(g_rope, 1.0)])])

    return {
        "norm_g": norm_g.reshape(1, D_MODEL),
        "w_cq": w_cq,
        "w_kvin": jnp.concatenate([col(1), w_kr, _half_swap(w_kr, -1.0)], axis=1).astype(BF16),
        "w_a": w_a,
        "w_b": w_b,
        "q_lora_g": q_lora_g.reshape(1, Q_RANK),
        "w_uq": jnp.concatenate([uq[:, :, :NOPE].reshape(Q_RANK, ATTN_WIDTH),
                                 uq_tiles.reshape(Q_RANK, ATTN_WIDTH)], axis=1).astype(BF16),
        "kv_lora_g": kv_lora_g.reshape(1, KV_RANK),
        "w_ukv": jnp.concatenate([ukv[:, :, :NOPE].reshape(KV_RANK, ATTN_WIDTH),
                                  ukv[:, :, NOPE:].reshape(KV_RANK, ATTN_WIDTH)], axis=1).astype(BF16),
        "q_head_g": head_gain(q_head_g),
        "k_head_g": head_gain(k_head_g),
        "w_gates": w_gates,
        "w_o_attn": w_o_attn.astype(BF16),
        "w_pw2": w_pw2.astype(BF16),
        "w_out": w_out.astype(BF16),
        "dw_packed": _pack_bf16_pairs(dw_kernel),
        "dw_bias": dw_bias.reshape(1, CONV_W),
        "conv_ln_g": conv_ln_g.reshape(1, CONV_W),
        "conv_ln_b": conv_ln_b.reshape(1, CONV_W),
    }


def _rope_table(seq_len):
    half = ROPE // 2
    inv_freq = 1.0 / (ROPE_THETA ** (jnp.arange(half, dtype=F32) / half))
    ang = jnp.arange(seq_len, dtype=F32)[:, None] * inv_freq[None, :]
    cos, sin = jnp.cos(ang), jnp.sin(ang)
    return jnp.concatenate([cos, cos, sin, sin], axis=1)


def _layer(x, w):
    cs = _rope_table(x.shape[1])
    q, k, v, u = _proj_call(x, cs, w)
    attn, c = _attn_call(q, k, v, u, w)
    return _merge_call(x, attn, c, w)


def kernel(x_prompt, x_sample, norm_g, w_in, q_lora_g, w_uq, kv_lora_g, w_ukv, q_head_g, k_head_g,
           w_o_attn, dw_kernel, dw_bias, conv_ln_g, conv_ln_b, w_pw2, w_out):
    depth = norm_g.shape[0]
    y_prompt, y_sample = x_prompt, x_sample
    for l in range(depth):
        w = _prep_weights(norm_g[l], w_in[l], q_lora_g[l], w_uq[l], kv_lora_g[l], w_ukv[l],
                          q_head_g[l], k_head_g[l], w_o_attn[l], dw_kernel[l], dw_bias[l],
                          conv_ln_g[l], conv_ln_b[l], w_pw2[l], w_out[l])
        y_prompt = _layer(y_prompt, w)
        y_sample = _layer(y_sample, w)
    return (y_prompt, y_sample)
```

```python
import math

import jax
import jax.numpy as jnp
from jax import lax
from jax.experimental import pallas as pl
from jax.experimental.pallas import tpu as pltpu

D_MODEL = 1024
N_HEADS = 8
NOPE = 128
ROPE = 64
QK_DIM = NOPE + ROPE
QK_PAD = 256
DV = 128
Q_RANK = 384
KV_RANK = 256
ATTN_WIDTH = N_HEADS * DV
CONV_W = D_MODEL
CONV_K = 31
CONV_PAD = (CONV_K - 1) // 2
ROPE_THETA = 10000.0
EPS = 1e-6
LANES = 128
HALO = 16
SPLITS = (Q_RANK, KV_RANK, ROPE, ATTN_WIDTH, 2 * CONV_W, CONV_W, D_MODEL, D_MODEL)

VMEM_LIMIT_BYTES = 60 * 1024 * 1024
TM_PROJ = 1024
TM_MERGE = 1024
TQ = 256
WSPLIT_STEPS = 8
HEADS_PER_STEP = 2
CONV_ROWS = 32
CONV_GROUP = 8
UBUF_EXTRA = 8

F32 = jnp.float32
BF16 = jnp.bfloat16
U32 = jnp.uint32

assert N_HEADS * LANES == CONV_W


def _const_spec(shape):
    return pl.BlockSpec(shape, lambda *_: (0,) * len(shape), pipeline_mode=pl.Buffered(1))


def _params():
    return pltpu.CompilerParams(dimension_semantics=("parallel", "parallel"),
                                vmem_limit_bytes=VMEM_LIMIT_BYTES)


def _proj_body(x_ref, cs_ref, ng_ref, wcq_ref, wkv_ref, wa_ref, wb_ref, qlg_ref, wuq_ref,
               kvlg_ref, wukv_ref, qg_ref, kg_ref, q_ref, k_ref, v_ref, u_ref):
    x = x_ref[0]
    inv = lax.rsqrt(jnp.mean(x * x, axis=-1, keepdims=True) + EPS)
    h = (x * inv * ng_ref[...]).astype(BF16)
    scale = math.log2(math.e) / math.sqrt(QK_DIM)
    cs = cs_ref[...]

    cq = jnp.dot(h, wcq_ref[...], preferred_element_type=F32)
    cq_inv = lax.rsqrt(jnp.mean(cq * cq, axis=-1, keepdims=True) + EPS)
    cqn = (cq * cq_inv * qlg_ref[...]).astype(BF16)
    qa = jnp.dot(cqn, wuq_ref[...], preferred_element_type=F32)
    q_cs = cs * qg_ref[1:2, :]
    for hh in range(N_HEADS):
        nope = qa[:, LANES * hh:LANES * (hh + 1)]
        rt = qa[:, ATTN_WIDTH + LANES * hh:ATTN_WIDTH + LANES * (hh + 1)]
        ss = jnp.sum(nope * nope + 0.5 * (rt * rt), axis=-1, keepdims=True)
        inv_h = lax.rsqrt(ss * (1.0 / QK_DIM) + EPS) * scale
        q_ref[0, hh, :, 0:LANES] = (nope * inv_h * qg_ref[0:1, :]).astype(BF16)
        t = rt * q_cs
        r = t + pltpu.roll(t, ROPE, 1)
        q_ref[0, hh, :, LANES:2 * LANES] = (r * inv_h).astype(BF16)

    kvin = jnp.dot(h, wkv_ref[...], preferred_element_type=F32)
    ckv = kvin[:, :KV_RANK]
    krt = kvin[:, KV_RANK:KV_RANK + LANES]
    ckv_inv = lax.rsqrt(jnp.mean(ckv * ckv, axis=-1, keepdims=True) + EPS)
    ckvn = (ckv * ckv_inv * kvlg_ref[...]).astype(BF16)
    kva = jnp.dot(ckvn, wukv_ref[...], preferred_element_type=F32)
    kr_sq = 0.5 * (krt * krt)
    tk = krt * (cs * kg_ref[1:2, :])
    rk = tk + pltpu.roll(tk, ROPE, 1)
    lane = lax.broadcasted_iota(jnp.int32, rk.shape, 1)
    rk = jnp.where(lane < ROPE, rk, 0.0)
    for hh in range(N_HEADS):
        kn = kva[:, LANES * hh:LANES * (hh + 1)]
        ss = jnp.sum(kn * kn + kr_sq, axis=-1, keepdims=True)
        inv_h = lax.rsqrt(ss * (1.0 / QK_DIM) + EPS)
        k_ref[0, hh, :, 0:LANES] = (kn * inv_h * kg_ref[0:1, :]).astype(BF16)
        k_ref[0, hh, :, LANES:2 * LANES] = (rk * inv_h).astype(BF16)
        v_ref[0, hh] = kva[:, ATTN_WIDTH + LANES * hh:ATTN_WIDTH + LANES * (hh + 1)].astype(BF16)

    a = jnp.dot(h, wa_ref[...], preferred_element_type=F32)
    b = jnp.dot(h, wb_ref[...], preferred_element_type=F32)
    u = (a * jax.nn.sigmoid(b)).astype(BF16)
    for hh in range(N_HEADS):
        u_ref[0, hh] = u[:, LANES * hh:LANES * (hh + 1)]


def _proj_call(x, cs, w):
    B, S, D = x.shape
    tm = TM_PROJ
    heads = lambda width: pl.BlockSpec((1, N_HEADS, tm, width), lambda b, i: (b, 0, i, 0))
    in_specs = [
        pl.BlockSpec((1, tm, D), lambda b, i: (b, i, 0)),
        pl.BlockSpec((tm, LANES), lambda b, i: (i, 0)),
        _const_spec((1, D)),
        _const_spec((D, Q_RANK)),
        _const_spec((D, KV_RANK + LANES)),
        _const_spec((D, CONV_W)),
        _const_spec((D, CONV_W)),
        _const_spec((1, Q_RANK)),
        _const_spec((Q_RANK, 2 * ATTN_WIDTH)),
        _const_spec((1, KV_RANK)),
        _const_spec((KV_RANK, 2 * ATTN_WIDTH)),
        _const_spec((2, LANES)),
        _const_spec((2, LANES)),
    ]
    out_specs = [heads(QK_PAD), heads(QK_PAD), heads(DV), heads(LANES)]
    out_shape = [
        jax.ShapeDtypeStruct((B, N_HEADS, S, QK_PAD), BF16),
        jax.ShapeDtypeStruct((B, N_HEADS, S, QK_PAD), BF16),
        jax.ShapeDtypeStruct((B, N_HEADS, S, DV), BF16),
        jax.ShapeDtypeStruct((B, N_HEADS, S, LANES), BF16),
    ]
    return pl.pallas_call(
        _proj_body,
        grid=(B, S // tm),
        in_specs=in_specs,
        out_specs=out_specs,
        out_shape=out_shape,
        compiler_params=_params(),
        name="mla_conv_proj",
    )(x, cs, w["norm_g"], w["w_cq"], w["w_kvin"], w["w_a"], w["w_b"], w["q_lora_g"], w["w_uq"],
      w["kv_lora_g"], w["w_ukv"], w["q_head_g"], w["k_head_g"])


def _attn_body(q_ref, k_ref, v_ref, u_ref, dwp_ref, dwb_ref, o_ref, c_ref,
               vaug_ref, s_ref, p_ref, m_ref, ubuf_ref, pair_ref):
    n_heads = k_ref.shape[1]
    S = k_ref.shape[2]
    tq = s_ref.shape[2]
    rounds_per_head = S // (2 * tq)
    n_words = S // 2 + HALO
    bits = lambda x: lax.bitcast_convert_type(x, U32)

    for hd in range(n_heads):
        vaug_ref[hd, :, 0:DV] = v_ref[0, hd]
        vaug_ref[hd, :, DV:2 * DV] = jnp.ones((S, DV), BF16)
        ubuf_ref[hd, 0:HALO, :] = jnp.zeros((HALO, LANES), F32)
        ubuf_ref[hd, HALO:HALO + S, :] = u_ref[0, hd].astype(F32)
        ubuf_ref[hd, HALO + S:2 * HALO + S + UBUF_EXTRA, :] = jnp.zeros((HALO + UBUF_EXTRA, LANES), F32)
        even = bits(ubuf_ref[hd, pl.ds(0, n_words, stride=2), :])
        odd = bits(ubuf_ref[hd, pl.ds(1, n_words, stride=2), :])
        even2 = bits(ubuf_ref[hd, pl.ds(2, n_words, stride=2), :])
        pair_ref[hd, 0] = (even >> 16) | odd
        pair_ref[hd, 1] = (odd >> 16) | even2

    chunks_per_block = S // CONV_ROWS // (2 * rounds_per_head)
    anchor_rows = tq // chunks_per_block
    assert chunks_per_block * 2 * rounds_per_head * CONV_ROWS == S and anchor_rows % 8 == 0

    def conv_chunk(hd, c):
        r0 = c * CONV_ROWS
        lanes = slice(LANES * hd, LANES * (hd + 1))

        def product(kk):
            start = r0 + (HALO - CONV_PAD) + kk
            win = pltpu.bitcast(pair_ref[hd, start % 2, pl.ds(start // 2, CONV_ROWS // 2), :], BF16)
            tap = jnp.broadcast_to(dwp_ref[kk:kk + 1, lanes], (CONV_ROWS // 2, LANES))
            return win * pltpu.bitcast(tap, BF16)

        def tree(terms):
            while len(terms) > 1:
                terms = [a + b for a, b in zip(terms[0::2], terms[1::2])] + terms[len(terms) & ~1:]
            return terms[0]

        out = dwb_ref[:, lanes]
        for g in range(0, CONV_K, CONV_GROUP):
            out = out + tree([product(kk) for kk in range(g, min(g + CONV_GROUP, CONV_K))]).astype(F32)
        c_ref[0, hd, r0:r0 + CONV_ROWS, :] = out
        return (bits(out[0:8, :]) >> 16) >> 16

    def split(rnd):
        return rnd // rounds_per_head, rnd % rounds_per_head

    def rows(r, j):
        return slice((2 * r + j) * tq, (2 * r + j + 1) * tq)

    def qk(rnd):
        hd, r = split(rnd)
        for j in range(2):
            s = lax.dot_general(q_ref[0, hd, rows(r, j), :], k_ref[0, hd],
                                (((1,), (1,)), ((), ())), preferred_element_type=F32)
            s_ref[rnd % 2, j] = s
            m_ref[rnd % 2, j] = jnp.max(s, axis=-1, keepdims=True)

    def softmax(rnd):
        hd, r = split(rnd)
        for j in range(2):
            for i in range(chunks_per_block):
                zero = conv_chunk(hd, (2 * r + j) * chunks_per_block + i)
                a0 = i * anchor_rows
                head = bits(s_ref[rnd % 2, j, a0:a0 + 8, 0:LANES])
                s_ref[rnd % 2, j, a0:a0 + 8, 0:LANES] = lax.bitcast_convert_type(head | zero, F32)
            s = s_ref[rnd % 2, j]
            p_ref[rnd % 2, j] = jnp.exp2(s - m_ref[rnd % 2, j]).astype(BF16)

    def pv(rnd):
        hd, r = split(rnd)
        os_ = [jnp.dot(p_ref[rnd % 2, j], vaug_ref[hd], preferred_element_type=F32) for j in range(2)]
        for j in range(2):
            o = os_[j]
            o_ref[0, hd, rows(r, j), :] = (o[:, 0:DV] / o[:, DV:2 * DV]).astype(BF16)

    n_rounds = n_heads * rounds_per_head
    qk(0)
    for rnd in range(n_rounds):
        if rnd + 1 < n_rounds:
            qk(rnd + 1)
        softmax(rnd)
        pv(rnd)


def _attn_call(q, k, v, u, w):
    B, H, S, _ = q.shape
    hps = HEADS_PER_STEP
    heads = lambda width: pl.BlockSpec((1, hps, S, width), lambda b, h: (b, h, 0, 0))
    return pl.pallas_call(
        _attn_body,
        grid=(B, H // hps),
        in_specs=[
            heads(QK_PAD), heads(QK_PAD), heads(DV), heads(LANES),
            pl.BlockSpec((CONV_K, hps * LANES), lambda b, h: (0, h)),
            pl.BlockSpec((1, hps * LANES), lambda b, h: (0, h)),
        ],
        out_specs=[heads(DV), heads(LANES)],
        out_shape=[jax.ShapeDtypeStruct((B, H, S, DV), BF16),
                   jax.ShapeDtypeStruct((B, H, S, LANES), F32)],
        scratch_shapes=[pltpu.VMEM((hps, S, 2 * DV), BF16),
                        pltpu.VMEM((2, 2, TQ, S), F32),
                        pltpu.VMEM((2, 2, TQ, S), BF16),
                        pltpu.VMEM((2, 2, TQ, 1), F32),
                        pltpu.VMEM((hps, S + 2 * HALO + UBUF_EXTRA, LANES), F32),
                        pltpu.VMEM((hps, 2, S // 2 + HALO, LANES), U32)],
        compiler_params=_params(),
        name="mla_attention_conv",
    )(q, k, v, u, w["dw_packed"], w["dw_bias"])


def _merge_body(x_ref, attn_ref, c_ref, ng_ref, wg_ref, wo_ref, wpw_ref, wout_ref,
                lng_ref, lnb_ref, o_ref):
    x = x_ref[0]
    inv = lax.rsqrt(jnp.mean(x * x, axis=-1, keepdims=True) + EPS)
    h = (x * inv * ng_ref[...]).astype(BF16)

    ga = jax.nn.silu(jnp.dot(h, wg_ref[:, 0:D_MODEL], preferred_element_type=F32))
    attn = jnp.concatenate([attn_ref[0, hh] for hh in range(N_HEADS)], axis=1)
    ya_in = (attn.astype(F32) * ga).astype(BF16)
    y_a = jnp.dot(ya_in, wo_ref[...], preferred_element_type=F32)

    c = jnp.concatenate([c_ref[0, hh] for hh in range(N_HEADS)], axis=1)
    mu = jnp.mean(c, axis=-1, keepdims=True)
    cc = c - mu
    var = jnp.mean(cc * cc, axis=-1, keepdims=True)
    ln = cc * lax.rsqrt(var + EPS) * lng_ref[...] + lnb_ref[...]
    gc = jax.nn.silu(jnp.dot(h, wg_ref[:, D_MODEL:2 * D_MODEL], preferred_element_type=F32))
    yc_in = (jax.nn.silu(ln) * gc).astype(BF16)
    y_c = jnp.dot(yc_in, wpw_ref[...], preferred_element_type=F32)

    ma = jax.nn.sigmoid(jnp.dot(h, wg_ref[:, 2 * D_MODEL:3 * D_MODEL], preferred_element_type=F32))
    mc = jax.nn.sigmoid(jnp.dot(h, wg_ref[:, 3 * D_MODEL:4 * D_MODEL], preferred_element_type=F32))
    merged = (ma * y_a + mc * y_c).astype(BF16)
    o_ref[0] = x + jnp.dot(merged, wout_ref[...], preferred_element_type=F32)


def _merge_call(x, attn, c, w):
    B, S, D = x.shape
    tm = TM_MERGE
    tile = lambda width: pl.BlockSpec((1, tm, width), lambda b, i: (b, i, 0))
    heads = lambda width: pl.BlockSpec((1, N_HEADS, tm, width), lambda b, i: (b, 0, i, 0))
    in_specs = [
        tile(D), heads(DV), heads(LANES),
        _const_spec((1, D)),
        _const_spec((D, 4 * D_MODEL)),
        _const_spec((ATTN_WIDTH, D_MODEL)),
        _const_spec((CONV_W, D_MODEL)),
        _const_spec((D_MODEL, D_MODEL)),
        _const_spec((1, CONV_W)),
        _const_spec((1, CONV_W)),
    ]
    return pl.pallas_call(
        _merge_body,
        grid=(B, S // tm),
        in_specs=in_specs,
        out_specs=tile(D),
        out_shape=jax.ShapeDtypeStruct((B, S, D), F32),
        compiler_params=_params(),
        name="gate_merge",
    )(x, attn, c, w["norm_g"], w["w_gates"], w["w_o_attn"], w["w_pw2"], w["w_out"],
      w["conv_ln_g"], w["conv_ln_b"])


def _wsplit_body(w_ref, wcq_ref, wkv_ref, wa_ref, wb_ref, wg_ref):
    offs = _split_offsets()
    cols = lambda g: slice(offs[g], offs[g + 1])
    wcq_ref[...] = w_ref[:, cols(0)].astype(BF16)
    wkv_ref[...] = w_ref[:, offs[1]:offs[3]]
    wa_ref[...] = w_ref[:, offs[4]:offs[4] + CONV_W].astype(BF16)
    wb_ref[...] = w_ref[:, offs[4] + CONV_W:offs[5]].astype(BF16)
    wg_ref[:, 0:D_MODEL] = w_ref[:, cols(3)].astype(BF16)
    wg_ref[:, D_MODEL:4 * D_MODEL] = w_ref[:, offs[5]:offs[8]].astype(BF16)


def _wsplit_call(w_in):
    rows = D_MODEL // WSPLIT_STEPS
    out_cols = (Q_RANK, KV_RANK + ROPE, CONV_W, CONV_W, 4 * D_MODEL)
    out_dtypes = (BF16, F32, BF16, BF16, BF16)
    return pl.pallas_call(
        _wsplit_body,
        grid=(WSPLIT_STEPS,),
        in_specs=[pl.BlockSpec((rows, w_in.shape[1]), lambda i: (i, 0))],
        out_specs=[pl.BlockSpec((rows, n), lambda i: (i, 0)) for n in out_cols],
        out_shape=[jax.ShapeDtypeStruct((D_MODEL, n), t) for n, t in zip(out_cols, out_dtypes)],
        compiler_params=pltpu.CompilerParams(dimension_semantics=("parallel",),
                                             vmem_limit_bytes=VMEM_LIMIT_BYTES),
        name="w_in_split",
    )(w_in)


def _half_swap(t, sign):
    half = t.shape[-1] // 2
    return jnp.concatenate([sign * t[..., half:], t[..., :half]], axis=-1)


def _pack_bf16_pairs(t):
    half = lax.bitcast_convert_type(t.astype(BF16), jnp.uint16).astype(U32)
    return half | (half << 16)


def _split_offsets():
    offs = [0]
    for nsz in SPLITS:
        offs.append(offs[-1] + nsz)
    return offs


def _prep_weights(norm_g, w_in, q_lora_g, w_uq, kv_lora_g, w_ukv, q_head_g, k_head_g, w_o_attn,
                  dw_kernel, dw_bias, conv_ln_g, conv_ln_b, w_pw2, w_out):
    w_cq, w_kv, w_a, w_b, w_gates = _wsplit_call(w_in)
    w_ckv, w_kr = w_kv[:, :KV_RANK], w_kv[:, KV_RANK:]
    uq = w_uq.reshape(Q_RANK, N_HEADS, QK_DIM)
    uq_rope = uq[:, :, NOPE:]
    uq_tiles = jnp.concatenate([uq_rope, _half_swap(uq_rope, -1.0)], axis=-1)
    ukv = w_ukv.reshape(KV_RANK, N_HEADS, NOPE + DV)

    def head_gain(g):
        g_rope = g[NOPE:]
        return jnp.stack([g[:NOPE], jnp.concatenate([g_rope, _half_swap(g_rope, 1.0)])])

    return {
        "norm_g": norm_g.reshape(1, D_MODEL),
        "w_cq": w_cq,
        "w_kvin": jnp.concatenate([w_ckv, w_kr, _half_swap(w_kr, -1.0)], axis=1).astype(BF16),
        "w_a": w_a,
        "w_b": w_b,
        "q_lora_g": q_lora_g.reshape(1, Q_RANK),
        "w_uq": jnp.concatenate([uq[:, :, :NOPE].reshape(Q_RANK, ATTN_WIDTH),
                                 uq_tiles.reshape(Q_RANK, ATTN_WIDTH)], axis=1).astype(BF16),
        "kv_lora_g": kv_lora_g.reshape(1, KV_RANK),
        "w_ukv": jnp.concatenate([ukv[:, :, :NOPE].reshape(KV_RANK, ATTN_WIDTH),
                                  ukv[:, :, NOPE:].reshape(KV_RANK, ATTN_WIDTH)], axis=1).astype(BF16),
        "q_head_g": head_gain(q_head_g),
        "k_head_g": head_gain(k_head_g),
        "w_gates": w_gates,
        "w_o_attn": w_o_attn.astype(BF16),
        "w_pw2": w_pw2.astype(BF16),
        "w_out": w_out.astype(BF16),
        "dw_packed": _pack_bf16_pairs(dw_kernel),
        "dw_bias": dw_bias.reshape(1, CONV_W),
        "conv_ln_g": conv_ln_g.reshape(1, CONV_W),
        "conv_ln_b": conv_ln_b.reshape(1, CONV_W),
    }


def _rope_table(seq_len):
    half = ROPE // 2
    inv_freq = 1.0 / (ROPE_THETA ** (jnp.arange(half, dtype=F32) / half))
    ang = jnp.arange(seq_len, dtype=F32)[:, None] * inv_freq[None, :]
    cos, sin = jnp.cos(ang), jnp.sin(ang)
    return jnp.concatenate([cos, cos, sin, sin], axis=1)


def _layer(x, w):
    cs = _rope_table(x.shape[1])
    q, k, v, u = _proj_call(x, cs, w)
    attn, c = _attn_call(q, k, v, u, w)
    return _merge_call(x, attn, c, w)


def kernel(x_prompt, x_sample, norm_g, w_in, q_lora_g, w_uq, kv_lora_g, w_ukv, q_head_g, k_head_g,
           w_o_attn, dw_kernel, dw_bias, conv_ln_g, conv_ln_b, w_pw2, w_out):
    depth = norm_g.shape[0]
    y_prompt, y_sample = x_prompt, x_sample
    for l in range(depth):
        w = _prep_weights(norm_g[l], w_in[l], q_lora_g[l], w_uq[l], kv_lora_g[l], w_ukv[l],
                          q_head_g[l], k_head_g[l], w_o_attn[l], dw_kernel[l], dw_bias[l],
                          conv_ln_g[l], conv_ln_b[l], w_pw2[l], w_out[l])
        y_prompt = _layer(y_prompt, w)
        y_sample = _layer(y_sample, w)
    return (y_prompt, y_sample)
```

```python
import math

import jax
import jax.numpy as jnp
from jax import lax
from jax.experimental import pallas as pl
from jax.experimental.pallas import tpu as pltpu

D_MODEL = 1024
N_HEADS = 8
NOPE = 128
ROPE = 64
QK_DIM = NOPE + ROPE
QK_PAD = 256
DV = 128
Q_RANK = 384
KV_RANK = 256
ATTN_WIDTH = N_HEADS * DV
CONV_W = D_MODEL
CONV_K = 31
CONV_PAD = (CONV_K - 1) // 2
ROPE_THETA = 10000.0
EPS = 1e-6
LANES = 128
HALO = 16
SPLITS = (Q_RANK, KV_RANK, ROPE, ATTN_WIDTH, 2 * CONV_W, CONV_W, D_MODEL, D_MODEL)

VMEM_LIMIT_BYTES = 60 * 1024 * 1024
TM_PROJ = 1024
TM_MERGE = 1024
TQ = 256
WSPLIT_STEPS = 8
HEADS_PER_STEP = 2
CONV_ROWS = 32
CONV_GROUP = 8
UBUF_EXTRA = 8

F32 = jnp.float32
BF16 = jnp.bfloat16
U32 = jnp.uint32

assert N_HEADS * LANES == CONV_W


def _const_spec(shape):
    return pl.BlockSpec(shape, lambda *_: (0,) * len(shape), pipeline_mode=pl.Buffered(1))


def _params():
    return pltpu.CompilerParams(dimension_semantics=("parallel", "parallel"),
                                vmem_limit_bytes=VMEM_LIMIT_BYTES)


def _proj_body(x_ref, cs_ref, ng_ref, wcq_ref, wkv_ref, wa_ref, wb_ref, qlg_ref, wuq_ref,
               kvlg_ref, wukv_ref, qg_ref, kg_ref, q_ref, k_ref, v_ref, u_ref):
    x = x_ref[0]
    inv = lax.rsqrt(jnp.mean(x * x, axis=-1, keepdims=True) + EPS)
    h = (x * inv * ng_ref[...]).astype(BF16)
    scale = math.log2(math.e) / math.sqrt(QK_DIM)
    cs = cs_ref[...]

    cq = jnp.dot(h, wcq_ref[...], preferred_element_type=F32)
    cq_inv = lax.rsqrt(jnp.mean(cq * cq, axis=-1, keepdims=True) + EPS)
    cqn = (cq * cq_inv * qlg_ref[...]).astype(BF16)
    qa = jnp.dot(cqn, wuq_ref[...], preferred_element_type=F32)
    q_cs = cs * qg_ref[1:2, :]
    for hh in range(N_HEADS):
        nope = qa[:, LANES * hh:LANES * (hh + 1)]
        rt = qa[:, ATTN_WIDTH + LANES * hh:ATTN_WIDTH + LANES * (hh + 1)]
        ss = jnp.sum(nope * nope + 0.5 * (rt * rt), axis=-1, keepdims=True)
        inv_h = lax.rsqrt(ss * (1.0 / QK_DIM) + EPS) * scale
        q_ref[0, hh, :, 0:LANES] = (nope * inv_h * qg_ref[0:1, :]).astype(BF16)
        t = rt * q_cs
        r = t + pltpu.roll(t, ROPE, 1)
        q_ref[0, hh, :, LANES:2 * LANES] = (r * inv_h).astype(BF16)

    kvin = jnp.dot(h, wkv_ref[...], preferred_element_type=F32)
    ckv = kvin[:, :KV_RANK]
    krt = kvin[:, KV_RANK:KV_RANK + LANES]
    ckv_inv = lax.rsqrt(jnp.mean(ckv * ckv, axis=-1, keepdims=True) + EPS)
    ckvn = (ckv * ckv_inv * kvlg_ref[...]).astype(BF16)
    kva = jnp.dot(ckvn, wukv_ref[...], preferred_element_type=F32)
    kr_sq = 0.5 * (krt * krt)
    tk = krt * (cs * kg_ref[1:2, :])
    rk = tk + pltpu.roll(tk, ROPE, 1)
    lane = lax.broadcasted_iota(jnp.int32, rk.shape, 1)
    rk = jnp.where(lane < ROPE, rk, 0.0)
    for hh in range(N_HEADS):
        kn = kva[:, LANES * hh:LANES * (hh + 1)]
        ss = jnp.sum(kn * kn + kr_sq, axis=-1, keepdims=True)
        inv_h = lax.rsqrt(ss * (1.0 / QK_DIM) + EPS)
        k_ref[0, hh, :, 0:LANES] = (kn * inv_h * kg_ref[0:1, :]).astype(BF16)
        k_ref[0, hh, :, LANES:2 * LANES] = (rk * inv_h).astype(BF16)
        v_ref[0, hh] = kva[:, ATTN_WIDTH + LANES * hh:ATTN_WIDTH + LANES * (hh + 1)].astype(BF16)

    a = jnp.dot(h, wa_ref[...], preferred_element_type=F32)
    b = jnp.dot(h, wb_ref[...], preferred_element_type=F32)
    u = (a * jax.nn.sigmoid(b)).astype(BF16)
    for hh in range(N_HEADS):
        u_ref[0, hh] = u[:, LANES * hh:LANES * (hh + 1)]


def _proj_call(x, cs, w):
    B, S, D = x.shape
    tm = TM_PROJ
    heads = lambda width: pl.BlockSpec((1, N_HEADS, tm, width), lambda b, i: (b, 0, i, 0))
    in_specs = [
        pl.BlockSpec((1, tm, D), lambda b, i: (b, i, 0)),
        pl.BlockSpec((tm, LANES), lambda b, i: (i, 0)),
        _const_spec((1, D)),
        _const_spec((D, Q_RANK)),
        _const_spec((D, KV_RANK + LANES)),
        _const_spec((D, CONV_W)),
        _const_spec((D, CONV_W)),
        _const_spec((1, Q_RANK)),
        _const_spec((Q_RANK, 2 * ATTN_WIDTH)),
        _const_spec((1, KV_RANK)),
        _const_spec((KV_RANK, 2 * ATTN_WIDTH)),
        _const_spec((2, LANES)),
        _const_spec((2, LANES)),
    ]
    out_specs = [heads(QK_PAD), heads(QK_PAD), heads(DV), heads(LANES)]
    out_shape = [
        jax.ShapeDtypeStruct((B, N_HEADS, S, QK_PAD), BF16),
        jax.ShapeDtypeStruct((B, N_HEADS, S, QK_PAD), BF16),
        jax.ShapeDtypeStruct((B, N_HEADS, S, DV), BF16),
        jax.ShapeDtypeStruct((B, N_HEADS, S, LANES), BF16),
    ]
    return pl.pallas_call(
        _proj_body,
        grid=(B, S // tm),
        in_specs=in_specs,
        out_specs=out_specs,
        out_shape=out_shape,
        compiler_params=_params(),
        name="mla_conv_proj",
    )(x, cs, w["norm_g"], w["w_cq"], w["w_kvin"], w["w_a"], w["w_b"], w["q_lora_g"], w["w_uq"],
      w["kv_lora_g"], w["w_ukv"], w["q_head_g"], w["k_head_g"])


def _attn_body(q_ref, k_ref, v_ref, u_ref, dwp_ref, dwb_ref, o_ref, c_ref,
               vaug_ref, s_ref, p_ref, m_ref, ubuf_ref, pair_ref):
    n_heads = k_ref.shape[1]
    S = k_ref.shape[2]
    tq = s_ref.shape[2]
    rounds_per_head = S // (2 * tq)
    n_words = S // 2 + HALO
    bits = lambda x: lax.bitcast_convert_type(x, U32)

    for hd in range(n_heads):
        vaug_ref[hd, :, 0:DV] = v_ref[0, hd]
        vaug_ref[hd, :, DV:2 * DV] = jnp.ones((S, DV), BF16)
        ubuf_ref[hd, 0:HALO, :] = jnp.zeros((HALO, LANES), F32)
        ubuf_ref[hd, HALO:HALO + S, :] = u_ref[0, hd].astype(F32)
        ubuf_ref[hd, HALO + S:2 * HALO + S + UBUF_EXTRA, :] = jnp.zeros((HALO + UBUF_EXTRA, LANES), F32)
        even = bits(ubuf_ref[hd, pl.ds(0, n_words, stride=2), :])
        odd = bits(ubuf_ref[hd, pl.ds(1, n_words, stride=2), :])
        even2 = bits(ubuf_ref[hd, pl.ds(2, n_words, stride=2), :])
        pair_ref[hd, 0] = (even >> 16) | odd
        pair_ref[hd, 1] = (odd >> 16) | even2

    chunks_per_block = S // CONV_ROWS // (2 * rounds_per_head)
    anchor_rows = tq // chunks_per_block
    assert chunks_per_block * 2 * rounds_per_head * CONV_ROWS == S and anchor_rows % 8 == 0

    def conv_chunk(hd, c):
        r0 = c * CONV_ROWS
        lanes = slice(LANES * hd, LANES * (hd + 1))

        def product(kk):
            start = r0 + (HALO - CONV_PAD) + kk
            win = pltpu.bitcast(pair_ref[hd, start % 2, pl.ds(start // 2, CONV_ROWS // 2), :], BF16)
            tap = jnp.broadcast_to(dwp_ref[kk:kk + 1, lanes], (CONV_ROWS // 2, LANES))
            return win * pltpu.bitcast(tap, BF16)

        def tree(terms):
            while len(terms) > 1:
                terms = [a + b for a, b in zip(terms[0::2], terms[1::2])] + terms[len(terms) & ~1:]
            return terms[0]

        out = dwb_ref[:, lanes]
        for g in range(0, CONV_K, CONV_GROUP):
            out = out + tree([product(kk) for kk in range(g, min(g + CONV_GROUP, CONV_K))]).astype(F32)
        c_ref[0, hd, r0:r0 + CONV_ROWS, :] = out.astype(BF16)
        return (bits(out[0:8, :]) >> 16) >> 16

    def split(rnd):
        return rnd // rounds_per_head, rnd % rounds_per_head

    def rows(r, j):
        return slice((2 * r + j) * tq, (2 * r + j + 1) * tq)

    def qk(rnd):
        hd, r = split(rnd)
        for j in range(2):
            s = lax.dot_general(q_ref[0, hd, rows(r, j), :], k_ref[0, hd],
                                (((1,), (1,)), ((), ())), preferred_element_type=F32)
            s_ref[rnd % 2, j] = s
            m_ref[rnd % 2, j] = jnp.max(s, axis=-1, keepdims=True)

    def softmax(rnd):
        hd, r = split(rnd)
        for j in range(2):
            for i in range(chunks_per_block):
                zero = conv_chunk(hd, (2 * r + j) * chunks_per_block + i)
                a0 = i * anchor_rows
                head = bits(s_ref[rnd % 2, j, a0:a0 + 8, 0:LANES])
                s_ref[rnd % 2, j, a0:a0 + 8, 0:LANES] = lax.bitcast_convert_type(head | zero, F32)
            s = s_ref[rnd % 2, j]
            p_ref[rnd % 2, j] = jnp.exp2(s - m_ref[rnd % 2, j]).astype(BF16)

    def pv(rnd):
        hd, r = split(rnd)
        os_ = [jnp.dot(p_ref[rnd % 2, j], vaug_ref[hd], preferred_element_type=F32) for j in range(2)]
        for j in range(2):
            o = os_[j]
            o_ref[0, hd, rows(r, j), :] = (o[:, 0:DV] / o[:, DV:2 * DV]).astype(BF16)

    n_rounds = n_heads * rounds_per_head
    qk(0)
    for rnd in range(n_rounds):
        if rnd + 1 < n_rounds:
            qk(rnd + 1)
        softmax(rnd)
        pv(rnd)


def _attn_call(q, k, v, u, w):
    B, H, S, _ = q.shape
    hps = HEADS_PER_STEP
    heads = lambda width: pl.BlockSpec((1, hps, S, width), lambda b, h: (b, h, 0, 0))
    return pl.pallas_call(
        _attn_body,
        grid=(B, H // hps),
        in_specs=[
            heads(QK_PAD), heads(QK_PAD), heads(DV), heads(LANES),
            pl.BlockSpec((CONV_K, hps * LANES), lambda b, h: (0, h)),
            pl.BlockSpec((1, hps * LANES), lambda b, h: (0, h)),
        ],
        out_specs=[heads(DV), heads(LANES)],
        out_shape=[jax.ShapeDtypeStruct((B, H, S, DV), BF16),
                   jax.ShapeDtypeStruct((B, H, S, LANES), BF16)],
        scratch_shapes=[pltpu.VMEM((hps, S, 2 * DV), BF16),
                        pltpu.VMEM((2, 2, TQ, S), F32),
                        pltpu.VMEM((2, 2, TQ, S), BF16),
                        pltpu.VMEM((2, 2, TQ, 1), F32),
                        pltpu.VMEM((hps, S + 2 * HALO + UBUF_EXTRA, LANES), F32),
                        pltpu.VMEM((hps, 2, S // 2 + HALO, LANES), U32)],
        compiler_params=_params(),
        name="mla_attention_conv",
    )(q, k, v, u, w["dw_packed"], w["dw_bias"])


def _merge_body(x_ref, attn_ref, c_ref, ng_ref, wg_ref, wo_ref, wpw_ref, wout_ref,
                lng_ref, lnb_ref, o_ref):
    x = x_ref[0]
    inv = lax.rsqrt(jnp.mean(x * x, axis=-1, keepdims=True) + EPS)
    h = (x * inv * ng_ref[...]).astype(BF16)

    ga = jax.nn.silu(jnp.dot(h, wg_ref[:, 0:D_MODEL], preferred_element_type=F32))
    attn = jnp.concatenate([attn_ref[0, hh] for hh in range(N_HEADS)], axis=1)
    ya_in = (attn.astype(F32) * ga).astype(BF16)
    y_a = jnp.dot(ya_in, wo_ref[...], preferred_element_type=F32)

    c = jnp.concatenate([c_ref[0, hh] for hh in range(N_HEADS)], axis=1).astype(F32)
    mu = jnp.mean(c, axis=-1, keepdims=True)
    cc = c - mu
    var = jnp.mean(cc * cc, axis=-1, keepdims=True)
    ln = cc * lax.rsqrt(var + EPS) * lng_ref[...] + lnb_ref[...]
    gc = jax.nn.silu(jnp.dot(h, wg_ref[:, D_MODEL:2 * D_MODEL], preferred_element_type=F32))
    yc_in = (jax.nn.silu(ln) * gc).astype(BF16)
    y_c = jnp.dot(yc_in, wpw_ref[...], preferred_element_type=F32)

    ma = jax.nn.sigmoid(jnp.dot(h, wg_ref[:, 2 * D_MODEL:3 * D_MODEL], preferred_element_type=F32))
    mc = jax.nn.sigmoid(jnp.dot(h, wg_ref[:, 3 * D_MODEL:4 * D_MODEL], preferred_element_type=F32))
    merged = (ma * y_a + mc * y_c).astype(BF16)
    o_ref[0] = x + jnp.dot(merged, wout_ref[...], preferred_element_type=F32)


def _merge_call(x, attn, c, w):
    B, S, D = x.shape
    tm = TM_MERGE
    tile = lambda width: pl.BlockSpec((1, tm, width), lambda b, i: (b, i, 0))
    heads = lambda width: pl.BlockSpec((1, N_HEADS, tm, width), lambda b, i: (b, 0, i, 0))
    in_specs = [
        tile(D), heads(DV), heads(LANES),
        _const_spec((1, D)),
        _const_spec((D, 4 * D_MODEL)),
        _const_spec((ATTN_WIDTH, D_MODEL)),
        _const_spec((CONV_W, D_MODEL)),
        _const_spec((D_MODEL, D_MODEL)),
        _const_spec((1, CONV_W)),
        _const_spec((1, CONV_W)),
    ]
    return pl.pallas_call(
        _merge_body,
        grid=(B, S // tm),
        in_specs=in_specs,
        out_specs=tile(D),
        out_shape=jax.ShapeDtypeStruct((B, S, D), F32),
        compiler_params=_params(),
        name="gate_merge",
    )(x, attn, c, w["norm_g"], w["w_gates"], w["w_o_attn"], w["w_pw2"], w["w_out"],
      w["conv_ln_g"], w["conv_ln_b"])


def _wsplit_body(w_ref, wcq_ref, wkv_ref, wa_ref, wb_ref, wg_ref):
    offs = _split_offsets()
    cols = lambda g: slice(offs[g], offs[g + 1])
    wcq_ref[...] = w_ref[:, cols(0)].astype(BF16)
    wkv_ref[...] = w_ref[:, offs[1]:offs[3]]
    wa_ref[...] = w_ref[:, offs[4]:offs[4] + CONV_W].astype(BF16)
    wb_ref[...] = w_ref[:, offs[4] + CONV_W:offs[5]].astype(BF16)
    wg_ref[:, 0:D_MODEL] = w_ref[:, cols(3)].astype(BF16)
    wg_ref[:, D_MODEL:4 * D_MODEL] = w_ref[:, offs[5]:offs[8]].astype(BF16)


def _wsplit_call(w_in):
    rows = D_MODEL // WSPLIT_STEPS
    out_cols = (Q_RANK, KV_RANK + ROPE, CONV_W, CONV_W, 4 * D_MODEL)
    out_dtypes = (BF16, F32, BF16, BF16, BF16)
    return pl.pallas_call(
        _wsplit_body,
        grid=(WSPLIT_STEPS,),
        in_specs=[pl.BlockSpec((rows, w_in.shape[1]), lambda i: (i, 0))],
        out_specs=[pl.BlockSpec((rows, n), lambda i: (i, 0)) for n in out_cols],
        out_shape=[jax.ShapeDtypeStruct((D_MODEL, n), t) for n, t in zip(out_cols, out_dtypes)],
        compiler_params=pltpu.CompilerParams(dimension_semantics=("parallel",),
                                             vmem_limit_bytes=VMEM_LIMIT_BYTES),
        name="w_in_split",
    )(w_in)


def _half_swap(t, sign):
    half = t.shape[-1] // 2
    return jnp.concatenate([sign * t[..., half:], t[..., :half]], axis=-1)


def _pack_bf16_pairs(t):
    half = lax.bitcast_convert_type(t.astype(BF16), jnp.uint16).astype(U32)
    return half | (half << 16)


def _split_offsets():
    offs = [0]
    for nsz in SPLITS:
        offs.append(offs[-1] + nsz)
    return offs


def _prep_weights(norm_g, w_in, q_lora_g, w_uq, kv_lora_g, w_ukv, q_head_g, k_head_g, w_o_attn,
                  dw_kernel, dw_bias, conv_ln_g, conv_ln_b, w_pw2, w_out):
    w_cq, w_kv, w_a, w_b, w_gates = _wsplit_call(w_in)
    w_ckv, w_kr = w_kv[:, :KV_RANK], w_kv[:, KV_RANK:]
    uq = w_uq.reshape(Q_RANK, N_HEADS, QK_DIM)
    uq_rope = uq[:, :, NOPE:]
    uq_tiles = jnp.concatenate([uq_rope, _half_swap(uq_rope, -1.0)], axis=-1)
    ukv = w_ukv.reshape(KV_RANK, N_HEADS, NOPE + DV)

    def head_gain(g):
        g_rope = g[NOPE:]
        return jnp.stack([g[:NOPE], jnp.concatenate([g_rope, _half_swap(g_rope, 1.0)])])

    return {
        "norm_g": norm_g.reshape(1, D_MODEL),
        "w_cq": w_cq,
        "w_kvin": jnp.concatenate([w_ckv, w_kr, _half_swap(w_kr, -1.0)], axis=1).astype(BF16),
        "w_a": w_a,
        "w_b": w_b,
        "q_lora_g": q_lora_g.reshape(1, Q_RANK),
        "w_uq": jnp.concatenate([uq[:, :, :NOPE].reshape(Q_RANK, ATTN_WIDTH),
                                 uq_tiles.reshape(Q_RANK, ATTN_WIDTH)], axis=1).astype(BF16),
        "kv_lora_g": kv_lora_g.reshape(1, KV_RANK),
        "w_ukv": jnp.concatenate([ukv[:, :, :NOPE].reshape(KV_RANK, ATTN_WIDTH),
                                  ukv[:, :, NOPE:].reshape(KV_RANK, ATTN_WIDTH)], axis=1).astype(BF16),
        "q_head_g": head_gain(q_head_g),
        "k_head_g": head_gain(k_head_g),
        "w_gates": w_gates,
        "w_o_attn": w_o_attn.astype(BF16),
        "w_pw2": w_pw2.astype(BF16),
        "w_out": w_out.astype(BF16),
        "dw_packed": _pack_bf16_pairs(dw_kernel),
        "dw_bias": dw_bias.reshape(1, CONV_W),
        "conv_ln_g": conv_ln_g.reshape(1, CONV_W),
        "conv_ln_b": conv_ln_b.reshape(1, CONV_W),
    }


def _rope_table(seq_len):
    half = ROPE // 2
    inv_freq = 1.0 / (ROPE_THETA ** (jnp.arange(half, dtype=F32) / half))
    ang = jnp.arange(seq_len, dtype=F32)[:, None] * inv_freq[None, :]
    cos, sin = jnp.cos(ang), jnp.sin(ang)
    return jnp.concatenate([cos, cos, sin, sin], axis=1)


def _layer(x, w):
    cs = _rope_table(x.shape[1])
    q, k, v, u = _proj_call(x, cs, w)
    attn, c = _attn_call(q, k, v, u, w)
    return _merge_call(x, attn, c, w)


def kernel(x_prompt, x_sample, norm_g, w_in, q_lora_g, w_uq, kv_lora_g, w_ukv, q_head_g, k_head_g,
           w_o_attn, dw_kernel, dw_bias, conv_ln_g, conv_ln_b, w_pw2, w_out):
    depth = norm_g.shape[0]
    y_prompt, y_sample = x_prompt, x_sample
    for l in range(depth):
        w = _prep_weights(norm_g[l], w_in[l], q_lora_g[l], w_uq[l], kv_lora_g[l], w_ukv[l],
                          q_head_g[l], k_head_g[l], w_o_attn[l], dw_kernel[l], dw_bias[l],
                          conv_ln_g[l], conv_ln_b[l], w_pw2[l], w_out[l])
        y_prompt = _layer(y_prompt, w)
        y_sample = _layer(y_sample, w)
    return (y_prompt, y_sample)
```
